```python
import jax
import jax.numpy as jnp
from jax import lax
import numpy as np

D_MODEL = 1024
BATCH = 1
SEQ = 16384
DEPTH = 1
DEC_BATCH = 4
DEC_SEQ = 4096
PAST_LEN = 128

GRID_W = 64
Q_BLOCK = 128
ROPE_THETA = 10000.0
EPS = 1e-6

A_HEADS = 8
A_KV_HEADS = 2
A_HEAD_DIM = 64
A_WIDTH = A_HEADS * A_HEAD_DIM

B_HEADS = 8
B_NOPE = 64
B_ROPE = 32
B_V = 64
B_Q_RANK = 384
B_KV_RANK = 256
B_WIDTH = B_HEADS * B_V

IN_SIZES = (A_HEADS * A_HEAD_DIM, A_KV_HEADS * A_HEAD_DIM, A_KV_HEADS * A_HEAD_DIM, A_WIDTH,
            B_Q_RANK, B_KV_RANK, B_ROPE, B_WIDTH, D_MODEL, D_MODEL)
IN_WIDTH = sum(IN_SIZES)

kernel_name = 'hybrid_gqa_mla_encoder'


def _split_points():
    pts, acc = [], 0
    for s in IN_SIZES[:-1]:
        acc += s
        pts.append(acc)
    return pts


def rms_norm(x, g):
    xf = x.astype(jnp.float32)
    y = xf * lax.rsqrt(jnp.mean(xf * xf, axis=-1, keepdims=True) + EPS)
    return (y * g.astype(jnp.float32)).astype(x.dtype)


def axial_rope_tables(n, d_rot):
    rows = n // GRID_W
    row_ids = jnp.repeat(jnp.arange(rows, dtype=jnp.float32), GRID_W)
    col_ids = jnp.tile(jnp.arange(GRID_W, dtype=jnp.float32), rows)
    d_axis = d_rot // 2
    inv = ROPE_THETA ** (-jnp.arange(0, d_axis, 2, dtype=jnp.float32) / d_axis)
    ang = jnp.concatenate([row_ids[:, None] * inv, col_ids[:, None] * inv], axis=-1)
    return jnp.cos(ang), jnp.sin(ang)


def apply_rope(x, cos, sin):
    half = x.shape[-1] // 2
    xf = x.astype(jnp.float32)
    x1, x2 = xf[..., :half], xf[..., half:]
    c = cos[None, :, None, :]
    s = sin[None, :, None, :]
    return jnp.concatenate([x1 * c - x2 * s, x2 * c + x1 * s], axis=-1).astype(x.dtype)


def block_attention(q, k, v, scale):
    bsz, n, h, dk = q.shape
    g = k.shape[2]
    r = h // g
    nb = n // Q_BLOCK
    qb = q.reshape(bsz, nb, Q_BLOCK, g, r, dk).transpose(1, 0, 2, 3, 4, 5)

    def one_block(qblk):
        s = jnp.einsum('bqgrd,bkgd->bgrqk', qblk, k, preferred_element_type=jnp.float32) * scale
        p = jax.nn.softmax(s, axis=-1).astype(v.dtype)
        return jnp.einsum('bgrqk,bkgd->bqgrd', p, v)

    o = lax.map(one_block, qb)
    return o.transpose(1, 0, 2, 3, 4, 5).reshape(bsz, n, h, v.shape[-1])


def encoder_layer(x, c, cos_a, sin_a, cos_b, sin_b, ada_w, ada_b, pre_g, post_g, w_in,
                  a_q_g, a_k_g, b_q_g, b_q_up, b_kv_g, b_kv_up, a_out, b_out, w_o):
    bsz, n, _ = x.shape
    mod = jax.nn.silu(c) @ ada_w + ada_b
    shift, scale, gate = jnp.split(mod[:, None, :], 3, axis=-1)
    h = rms_norm(x, pre_g) * (1.0 + scale) + shift
    aq, ak, av, ag, bq, bkv, bkr, bg, ma, mb = jnp.split(h @ w_in, _split_points(), axis=-1)

    qa = apply_rope(rms_norm(aq.reshape(bsz, n, A_HEADS, A_HEAD_DIM), a_q_g), cos_a, sin_a)
    ka = apply_rope(rms_norm(ak.reshape(bsz, n, A_KV_HEADS, A_HEAD_DIM), a_k_g), cos_a, sin_a)
    va = av.reshape(bsz, n, A_KV_HEADS, A_HEAD_DIM)
    ya = block_attention(qa, ka, va, A_HEAD_DIM ** -0.5).reshape(bsz, n, A_WIDTH) * jax.nn.silu(ag)

    qb = (rms_norm(bq, b_q_g) @ b_q_up).reshape(bsz, n, B_HEADS, B_NOPE + B_ROPE)
    qb = jnp.concatenate([qb[..., :B_NOPE], apply_rope(qb[..., B_NOPE:], cos_b, sin_b)], axis=-1)
    kv = (rms_norm(bkv, b_kv_g) @ b_kv_up).reshape(bsz, n, B_HEADS, B_NOPE + B_V)
    kr = apply_rope(bkr.reshape(bsz, n, 1, B_ROPE), cos_b, sin_b)
    kb = jnp.concatenate([kv[..., :B_NOPE], jnp.broadcast_to(kr, (bsz, n, B_HEADS, B_ROPE))], axis=-1)
    vb = kv[..., B_NOPE:]
    yb = block_attention(qb, kb, vb, (B_NOPE + B_ROPE) ** -0.5).reshape(bsz, n, B_WIDTH) * jax.nn.silu(bg)

    merged = jax.nn.sigmoid(ma) * (ya @ a_out) + jax.nn.sigmoid(mb) * (yb @ b_out)
    return x + gate * rms_norm(merged @ w_o, post_g)


def trunk(x, c, ada_w, ada_b, pre_norm_g, post_norm_g, w_in, a_q_norm_g, a_k_norm_g,
          b_q_norm_g, b_q_up, b_kv_norm_g, b_kv_up, a_out, b_out, w_o):
    n = x.shape[1]
    cos_a, sin_a = axial_rope_tables(n, A_HEAD_DIM)
    cos_b, sin_b = axial_rope_tables(n, B_ROPE)
    for l in range(DEPTH):
        x = encoder_layer(x, c, cos_a, sin_a, cos_b, sin_b, ada_w[l], ada_b[l], pre_norm_g[l],
                          post_norm_g[l], w_in[l], a_q_norm_g[l], a_k_norm_g[l], b_q_norm_g[l],
                          b_q_up[l], b_kv_norm_g[l], b_kv_up[l], a_out[l], b_out[l], w_o[l])
    return x


def setup_inputs(seed: int = 0) -> dict:
    key = jax.random.key(seed)
    ks = jax.random.split(key, 20)
    f32 = jnp.float32

    def nrm(k, shape, s):
        return jax.random.normal(k, shape, f32) * s

    def gain(k, d):
        return 1.0 + 0.02 * jax.random.normal(k, (DEPTH, d), f32)

    return {
        'x_prompt': jax.random.normal(ks[0], (BATCH, SEQ, D_MODEL), f32),
        'x_sample': jax.random.normal(ks[1], (DEC_BATCH, DEC_SEQ, D_MODEL), f32),
        'c_prompt': jax.random.normal(ks[2], (BATCH, D_MODEL), f32),
        'c_sample': jax.random.normal(ks[3], (DEC_BATCH, D_MODEL), f32),
        'ada_w': nrm(ks[4], (DEPTH, D_MODEL, 3 * D_MODEL), 0.2 * D_MODEL ** -0.5),
        'ada_b': nrm(ks[5], (DEPTH, 3 * D_MODEL), 0.02),
        'pre_norm_g': gain(ks[6], D_MODEL),
        'post_norm_g': gain(ks[7], D_MODEL),
        'w_in': nrm(ks[8], (DEPTH, D_MODEL, IN_WIDTH), D_MODEL ** -0.5),
        'a_q_norm_g': gain(ks[9], A_HEAD_DIM),
        'a_k_norm_g': gain(ks[10], A_HEAD_DIM),
        'b_q_norm_g': gain(ks[11], B_Q_RANK),
        'b_q_up': nrm(ks[12], (DEPTH, B_Q_RANK, B_HEADS * (B_NOPE + B_ROPE)), B_Q_RANK ** -0.5),
        'b_kv_norm_g': gain(ks[13], B_KV_RANK),
        'b_kv_up': nrm(ks[14], (DEPTH, B_KV_RANK, B_HEADS * (B_NOPE + B_V)), B_KV_RANK ** -0.5),
        'a_out': nrm(ks[15], (DEPTH, A_WIDTH, D_MODEL), A_WIDTH ** -0.5),
        'b_out': nrm(ks[16], (DEPTH, B_WIDTH, D_MODEL), B_WIDTH ** -0.5),
        'w_o': nrm(ks[17], (DEPTH, D_MODEL, D_MODEL), D_MODEL ** -0.5),
    }


def reference(x_prompt, x_sample, c_prompt, c_sample, ada_w, ada_b, pre_norm_g, post_norm_g,
              w_in, a_q_norm_g, a_k_norm_g, b_q_norm_g, b_q_up, b_kv_norm_g, b_kv_up,
              a_out, b_out, w_o):
    y_prompt = trunk(x_prompt, c_prompt, ada_w, ada_b, pre_norm_g, post_norm_g, w_in,
                     a_q_norm_g, a_k_norm_g, b_q_norm_g, b_q_up, b_kv_norm_g, b_kv_up,
                     a_out, b_out, w_o)
    y_sample = trunk(x_sample, c_sample, ada_w, ada_b, pre_norm_g, post_norm_g, w_in,
                     a_q_norm_g, a_k_norm_g, b_q_norm_g, b_q_up, b_kv_norm_g, b_kv_up,
                     a_out, b_out, w_o)
    return (y_prompt, y_sample)
```

```python
import functools

import jax
import jax.numpy as jnp
from jax import lax
from jax.experimental import pallas as pl
from jax.experimental.pallas import tpu as pltpu

D_MODEL = 1024
GRID_W = 64
ROPE_THETA = 10000.0
EPS = 1e-6

A_HEADS = 8
A_KV_HEADS = 2
A_HEAD_DIM = 64
A_WIDTH = A_HEADS * A_HEAD_DIM

B_HEADS = 8
B_NOPE = 64
B_ROPE = 32
B_V = 64
B_Q_RANK = 384
B_KV_RANK = 256
B_WIDTH = B_HEADS * B_V

IN_SIZES = (A_WIDTH, A_KV_HEADS * A_HEAD_DIM, A_KV_HEADS * A_HEAD_DIM, A_WIDTH,
            B_Q_RANK, B_KV_RANK, B_ROPE, B_WIDTH, D_MODEL, D_MODEL)

LOG2E = 1.4426950408889634
LANES = 128
QK_PAD = 128
V_ROWS = 64
L_ROWS = 16
PROJ_TILE = 256
ATTN_TQ = 512
ATTN_TK = 512
VMEM_LIMIT = 56 * 1024 * 1024

F32 = jnp.float32
BF16 = jnp.bfloat16


def _rope_tables_t(n, d_rot):
    rows = n // GRID_W
    row_ids = jnp.repeat(jnp.arange(rows, dtype=F32), GRID_W)
    col_ids = jnp.tile(jnp.arange(GRID_W, dtype=F32), rows)
    d_axis = d_rot // 2
    inv = ROPE_THETA ** (-jnp.arange(0, d_axis, 2, dtype=F32) / d_axis)
    ang = jnp.concatenate([row_ids[:, None] * inv, col_ids[:, None] * inv], axis=-1)
    return jnp.cos(ang).T, jnp.sin(ang).T


def _const_spec(shape):
    nd = len(shape)
    return pl.BlockSpec(shape, lambda *_: (0,) * nd, pipeline_mode=pl.Buffered(1))


def _mod_kernel(c_ref, w_ref, b_ref, o_ref):
    c = c_ref[...]
    sc = (c * jax.nn.sigmoid(c)).astype(BF16)
    o_ref[...] = jnp.dot(sc, w_ref[...], preferred_element_type=F32) + b_ref[...]


def _mod_call(c_pad, ada_w, ada_b):
    rows = c_pad.shape[0]
    return pl.pallas_call(
        _mod_kernel,
        out_shape=jax.ShapeDtypeStruct((rows, 3 * D_MODEL), F32),
        name="adaln_mod",
    )(c_pad, ada_w, ada_b)


def _rope_t(x, cos, sin):
    half = x.shape[0] // 2
    x1, x2 = x[:half], x[half:]
    return jnp.concatenate([x1 * cos - x2 * sin, x2 * cos + x1 * sin], axis=0)


def _rms_t(x, g_col):
    ms = jnp.mean(x * x, axis=0, keepdims=True)
    return x * lax.rsqrt(ms + EPS) * g_col


def _proj_kernel(x_ref, mod_ref, pre_g_ref, wt_ref, wn_ref, aqg_ref, akg_ref, bqg_ref, bkvg_ref,
                 bqup_ref, waug_ref, bvup_ref, cos_a_ref, sin_a_ref, cos_b_ref, sin_b_ref,
                 qa_ref, ka_ref, va_ref, ag_ref, bg_ref, ma_ref, mb_ref, qb_ref, kb_ref, vb_ref):
    t = x_ref.shape[1]
    x = x_ref[0]
    shift = mod_ref[0, 0:1, :]
    scale = mod_ref[0, 1:2, :]
    ms = jnp.mean(x * x, axis=-1, keepdims=True)
    h = (x * lax.rsqrt(ms + EPS) * pre_g_ref[...]) * (1.0 + scale) + shift
    hb = h.astype(BF16)

    t1 = lax.dot_general(wt_ref[...], hb, (((1,), (1,)), ((), ())), preferred_element_type=F32)
    cos_a, sin_a = cos_a_ref[...], sin_a_ref[...]
    cos_b, sin_b = cos_b_ref[...], sin_b_ref[...]
    aqg, akg = aqg_ref[...], akg_ref[...]

    q_scale = (A_HEAD_DIM ** -0.5) * LOG2E
    for hd in range(A_HEADS):
        q = t1[hd * A_HEAD_DIM:(hd + 1) * A_HEAD_DIM]
        q = _rope_t(_rms_t(q, aqg), cos_a, sin_a) * q_scale
        qa_ref[0, hd * A_HEAD_DIM:(hd + 1) * A_HEAD_DIM, :] = q.astype(BF16)
    off = A_WIDTH
    for g in range(A_KV_HEADS):
        k = t1[off + g * A_HEAD_DIM:off + (g + 1) * A_HEAD_DIM]
        k = _rope_t(_rms_t(k, akg), cos_a, sin_a)
        slab = jnp.concatenate([k, jnp.zeros((QK_PAD - A_HEAD_DIM, t), F32)], axis=0)
        ka_ref[0, g] = slab.T.astype(BF16)
    off += A_KV_HEADS * A_HEAD_DIM
    for g in range(A_KV_HEADS):
        va_ref[0, g, 0] = t1[off + g * A_HEAD_DIM:off + (g + 1) * A_HEAD_DIM].astype(BF16)
    off += A_KV_HEADS * A_HEAD_DIM

    bqn = _rms_t(t1[off:off + B_Q_RANK], bqg_ref[...]).astype(BF16)
    off += B_Q_RANK
    qb = jnp.dot(bqup_ref[...], bqn, preferred_element_type=F32)
    qb_scale = ((B_NOPE + B_ROPE) ** -0.5) * LOG2E
    for hd in range(B_HEADS):
        blk = qb[hd * QK_PAD:(hd + 1) * QK_PAD]
        roped = _rope_t(blk[B_NOPE:B_NOPE + B_ROPE], cos_b, sin_b)
        full = jnp.concatenate([blk[:B_NOPE], roped, blk[B_NOPE + B_ROPE:]], axis=0) * qb_scale
        qb_ref[0, hd * QK_PAD:(hd + 1) * QK_PAD, :] = full.astype(BF16)

    kr = _rope_t(t1[off:off + B_ROPE], cos_b, sin_b)
    kr_slab = jnp.concatenate([kr, jnp.zeros((LANES - B_ROPE, t), F32)], axis=0).T

    bkv = jnp.dot(hb, wn_ref[:, 0:B_KV_RANK], preferred_element_type=F32)
    ms = jnp.mean(bkv * bkv, axis=-1, keepdims=True)
    bkvn = bkv * lax.rsqrt(ms + EPS) * bkvg_ref[...]
    lhs = jnp.concatenate([bkvn, kr_slab], axis=1).astype(BF16)
    kb_ref[0] = jnp.dot(lhs, waug_ref[...], preferred_element_type=F32).astype(BF16)
    vb = lax.dot_general(bvup_ref[...], bkvn.astype(BF16), (((1,), (1,)), ((), ())),
                         preferred_element_type=F32)
    for hd in range(B_HEADS):
        vb_ref[0, hd, 0] = vb[hd * B_V:(hd + 1) * B_V].astype(BF16)

    c0 = B_KV_RANK
    ag = jnp.dot(hb, wn_ref[:, c0:c0 + A_WIDTH], preferred_element_type=F32)
    ag_ref[0] = (ag * jax.nn.sigmoid(ag)).astype(BF16)
    c0 += A_WIDTH
    bg = jnp.dot(hb, wn_ref[:, c0:c0 + B_WIDTH], preferred_element_type=F32)
    bg_ref[0] = (bg * jax.nn.sigmoid(bg)).astype(BF16)
    c0 += B_WIDTH
    ma = jnp.dot(hb, wn_ref[:, c0:c0 + D_MODEL], preferred_element_type=F32)
    ma_ref[0] = jax.nn.sigmoid(ma).astype(BF16)
    c0 += D_MODEL
    mb = jnp.dot(hb, wn_ref[:, c0:c0 + D_MODEL], preferred_element_type=F32)
    mb_ref[0] = jax.nn.sigmoid(mb).astype(BF16)


def _proj_call(x, mod3, pw, tables):
    bsz, n, _ = x.shape
    t = PROJ_TILE
    nt = n // t
    cos_a, sin_a, cos_b, sin_b = tables
    tok = lambda b, i: (b, i, 0)
    feat = lambda b, i: (b, 0, i)
    in_specs = [
        pl.BlockSpec((1, t, D_MODEL), tok),
        pl.BlockSpec((1, 3, D_MODEL), lambda b, i: (b, 0, 0)),
        _const_spec((1, D_MODEL)),
        _const_spec(pw["wt"].shape),
        _const_spec(pw["wn"].shape),
        _const_spec((A_HEAD_DIM, 1)),
        _const_spec((A_HEAD_DIM, 1)),
        _const_spec((B_Q_RANK, 1)),
        _const_spec((1, B_KV_RANK)),
        _const_spec(pw["bqup"].shape),
        _const_spec(pw["waug"].shape),
        _const_spec(pw["bvup"].shape),
        pl.BlockSpec((A_HEAD_DIM // 2, t), lambda b, i: (0, i)),
        pl.BlockSpec((A_HEAD_DIM // 2, t), lambda b, i: (0, i)),
        pl.BlockSpec((B_ROPE // 2, t), lambda b, i: (0, i)),
        pl.BlockSpec((B_ROPE // 2, t), lambda b, i: (0, i)),
    ]
    out_shape = [
        jax.ShapeDtypeStruct((bsz, A_WIDTH, n), BF16),
        jax.ShapeDtypeStruct((bsz, A_KV_HEADS, n, QK_PAD), BF16),
        jax.ShapeDtypeStruct((bsz, A_KV_HEADS, nt, V_ROWS, t), BF16),
        jax.ShapeDtypeStruct((bsz, n, A_WIDTH), BF16),
        jax.ShapeDtypeStruct((bsz, n, B_WIDTH), BF16),
        jax.ShapeDtypeStruct((bsz, n, D_MODEL), BF16),
        jax.ShapeDtypeStruct((bsz, n, D_MODEL), BF16),
        jax.ShapeDtypeStruct((bsz, B_HEADS * QK_PAD, n), BF16),
        jax.ShapeDtypeStruct((bsz, n, B_HEADS * QK_PAD), BF16),
        jax.ShapeDtypeStruct((bsz, B_HEADS, nt, V_ROWS, t), BF16),
    ]
    out_specs = [
        pl.BlockSpec((1, A_WIDTH, t), feat),
        pl.BlockSpec((1, A_KV_HEADS, t, QK_PAD), lambda b, i: (b, 0, i, 0)),
        pl.BlockSpec((1, A_KV_HEADS, 1, V_ROWS, t), lambda b, i: (b, 0, i, 0, 0)),
        pl.BlockSpec((1, t, A_WIDTH), tok),
        pl.BlockSpec((1, t, B_WIDTH), tok),
        pl.BlockSpec((1, t, D_MODEL), tok),
        pl.BlockSpec((1, t, D_MODEL), tok),
        pl.BlockSpec((1, B_HEADS * QK_PAD, t), feat),
        pl.BlockSpec((1, t, B_HEADS * QK_PAD), tok),
        pl.BlockSpec((1, B_HEADS, 1, V_ROWS, t), lambda b, i: (b, 0, i, 0, 0)),
    ]
    return pl.pallas_call(
        _proj_kernel,
        grid=(bsz, nt),
        in_specs=in_specs,
        out_specs=out_specs,
        out_shape=out_shape,
        compiler_params=pltpu.CompilerParams(
            dimension_semantics=("arbitrary", "arbitrary"), vmem_limit_bytes=VMEM_LIMIT),
        name="in_proj",
    )(x, mod3, pw["pre_g"], pw["wt"], pw["wn"], pw["aqg"], pw["akg"], pw["bqg"], pw["bkvg"],
      pw["bqup"], pw["waug"], pw["bvup"], cos_a, sin_a, cos_b, sin_b)


def _attn_kernel(q_ref, k_ref, v_ref, o_ref, *, n_steps, k_4d):
    q = q_ref[0]
    if q.shape[0] < QK_PAD:
        q = jnp.concatenate([q, jnp.zeros((QK_PAD - q.shape[0], q.shape[1]), q.dtype)], axis=0)
    tq = q.shape[1]
    sub = ATTN_TK // PROJ_TILE
    ones = jnp.ones((L_ROWS, ATTN_TK), BF16)

    def body(j, carry):
        m, acc = carry
        off = pl.multiple_of(j * ATTN_TK, ATTN_TK)
        if k_4d:
            k = k_ref[0, 0, pl.ds(off, ATTN_TK), :]
        else:
            k = k_ref[0, pl.ds(off, ATTN_TK), :]
        s = jnp.dot(k, q, preferred_element_type=F32)
        m_new = jnp.maximum(m, jnp.max(s, axis=0, keepdims=True))
        p = jnp.exp2(s - m_new).astype(BF16)
        alpha = jnp.exp2(m - m_new)
        v = jnp.concatenate([v_ref[0, 0, j * sub + c] for c in range(sub)], axis=1)
        va = jnp.concatenate([v, ones], axis=0)
        acc = alpha * acc + jnp.dot(va, p, preferred_element_type=F32)
        return m_new, acc

    m0 = jnp.full((1, tq), -1e30, F32)
    acc0 = jnp.zeros((V_ROWS + L_ROWS, tq), F32)
    _, acc = lax.fori_loop(0, n_steps, body, (m0, acc0))
    o_ref[0] = acc[:V_ROWS] / acc[V_ROWS:V_ROWS + 1]


def _attn_call(q, k, v, *, heads, q_rows, group, k_4d, name):
    bsz, _, n = q.shape
    tq = ATTN_TQ
    n_chunks = v.shape[2]
    if k_4d:
        k_spec = pl.BlockSpec((1, 1, n, QK_PAD), lambda b, h, i: (b, h // group, 0, 0))
    else:
        k_spec = pl.BlockSpec((1, n, QK_PAD), lambda b, h, i: (b, 0, h))
    kernel = functools.partial(_attn_kernel, n_steps=n // ATTN_TK, k_4d=k_4d)
    return pl.pallas_call(
        kernel,
        grid=(bsz, heads, n // tq),
        in_specs=[
            pl.BlockSpec((1, q_rows, tq), lambda b, h, i: (b, h, i)),
            k_spec,
            pl.BlockSpec((1, 1, n_chunks, V_ROWS, PROJ_TILE), lambda b, h, i: (b, h // group, 0, 0, 0)),
        ],
        out_specs=pl.BlockSpec((1, V_ROWS, tq), lambda b, h, i: (b, h, i)),
        out_shape=jax.ShapeDtypeStruct((bsz, heads * V_ROWS, n), F32),
        compiler_params=pltpu.CompilerParams(
            dimension_semantics=("arbitrary", "arbitrary", "arbitrary"), vmem_limit_bytes=VMEM_LIMIT),
        name=name,
    )(q, k, v)


def _out_kernel(x_ref, mod_ref, post_g_ref, ya_ref, yb_ref, ag_ref, bg_ref, ma_ref, mb_ref,
                aout_ref, bout_ref, wo_ref, y_ref):
    ga = (ya_ref[0].T * ag_ref[0].astype(F32)).astype(BF16)
    gb = (yb_ref[0].T * bg_ref[0].astype(F32)).astype(BF16)
    pa = jnp.dot(ga, aout_ref[...], preferred_element_type=F32)
    pb = jnp.dot(gb, bout_ref[...], preferred_element_type=F32)
    merged = ma_ref[0].astype(F32) * pa + mb_ref[0].astype(F32) * pb
    z = jnp.dot(merged.astype(BF16), wo_ref[...], preferred_element_type=F32)
    ms = jnp.mean(z * z, axis=-1, keepdims=True)
    zn = z * lax.rsqrt(ms + EPS) * post_g_ref[...]
    y_ref[0] = x_ref[0] + mod_ref[0, 2:3, :] * zn


def _out_call(x, mod3, post_g, ya, yb, ag, bg, ma, mb, a_out, b_out, w_o):
    bsz, n, _ = x.shape
    t = PROJ_TILE
    tok = lambda b, i: (b, i, 0)
    feat = lambda b, i: (b, 0, i)
    return pl.pallas_call(
        _out_kernel,
        grid=(bsz, n // t),
        in_specs=[
            pl.BlockSpec((1, t, D_MODEL), tok),
            pl.BlockSpec((1, 3, D_MODEL), lambda b, i: (b, 0, 0)),
            _const_spec((1, D_MODEL)),
            pl.BlockSpec((1, A_WIDTH, t), feat),
            pl.BlockSpec((1, B_WIDTH, t), feat),
            pl.BlockSpec((1, t, A_WIDTH), tok),
            pl.BlockSpec((1, t, B_WIDTH), tok),
            pl.BlockSpec((1, t, D_MODEL), tok),
            pl.BlockSpec((1, t, D_MODEL), tok),
            _const_spec(a_out.shape),
            _const_spec(b_out.shape),
            _const_spec(w_o.shape),
        ],
        out_specs=pl.BlockSpec((1, t, D_MODEL), tok),
        out_shape=jax.ShapeDtypeStruct((bsz, n, D_MODEL), F32),
        compiler_params=pltpu.CompilerParams(
            dimension_semantics=("arbitrary", "arbitrary"), vmem_limit_bytes=VMEM_LIMIT),
        name="out_proj",
    )(x, mod3, post_g, ya, yb, ag, bg, ma, mb, a_out, b_out, w_o)


def _prep_weights(pre_g, w_in, a_q_g, a_k_g, b_q_g, b_q_up, b_kv_g, b_kv_up):
    pts = [0]
    for s in IN_SIZES:
        pts.append(pts[-1] + s)
    seg = lambda i: w_in[:, pts[i]:pts[i + 1]]
    aq, ak, av, ag, bq, bkv, bkr, bg, ma, mb = (seg(i) for i in range(10))
    wt = jnp.concatenate([aq, ak, av, bq, bkr], axis=1).T.astype(BF16)
    wn = jnp.concatenate([bkv, ag, bg, ma, mb], axis=1).astype(BF16)

    qd = B_NOPE + B_ROPE
    bqup = b_q_up.reshape(B_Q_RANK, B_HEADS, qd)
    bqup = jnp.pad(bqup, ((0, 0), (0, 0), (0, QK_PAD - qd))).reshape(B_Q_RANK, B_HEADS * QK_PAD)

    kvup = b_kv_up.reshape(B_KV_RANK, B_HEADS, B_NOPE + B_V)
    k_part = jnp.pad(kvup[:, :, :B_NOPE], ((0, 0), (0, 0), (0, QK_PAD - B_NOPE)))
    eye = jnp.eye(LANES, QK_PAD, k=B_NOPE, dtype=F32)[:, None, :]
    eye = jnp.where(jnp.arange(LANES)[:, None, None] < B_ROPE, eye, 0.0)
    r_part = jnp.broadcast_to(eye, (LANES, B_HEADS, QK_PAD))
    waug = jnp.concatenate([k_part, r_part], axis=0).reshape(B_KV_RANK + LANES, B_HEADS * QK_PAD)
    bvup = kvup[:, :, B_NOPE:].reshape(B_KV_RANK, B_HEADS * B_V).T

    return {
        "pre_g": pre_g.reshape(1, D_MODEL),
        "wt": wt, "wn": wn,
        "aqg": a_q_g.reshape(A_HEAD_DIM, 1), "akg": a_k_g.reshape(A_HEAD_DIM, 1),
        "bqg": b_q_g.reshape(B_Q_RANK, 1), "bkvg": b_kv_g.reshape(1, B_KV_RANK),
        "bqup": bqup.T.astype(BF16), "waug": waug.astype(BF16), "bvup": bvup.astype(BF16),
    }


def _layer(x, mod3, pw, post_g, a_out, b_out, w_o):
    n = x.shape[1]
    tables = _rope_tables_t(n, A_HEAD_DIM) + _rope_tables_t(n, B_ROPE)
    qa, ka, va, ag, bg, ma, mb, qb, kb, vb = _proj_call(x, mod3, pw, tables)
    ya = _attn_call(qa, ka, va, heads=A_HEADS, q_rows=A_HEAD_DIM, group=A_HEADS // A_KV_HEADS,
                    k_4d=True, name="attn_gqa")
    yb = _attn_call(qb, kb, vb, heads=B_HEADS, q_rows=QK_PAD, group=1, k_4d=False, name="attn_mla")
    return _out_call(x, mod3, post_g, ya, yb, ag, bg, ma, mb, a_out, b_out, w_o)


def kernel(x_prompt, x_sample, c_prompt, c_sample, ada_w, ada_b, pre_norm_g, post_norm_g, w_in,
           a_q_norm_g, a_k_norm_g, b_q_norm_g, b_q_up, b_kv_norm_g, b_kv_up, a_out, b_out, w_o):
    assert ada_w.shape[0] == 1, "single layer"
    bp, bs = c_prompt.shape[0], c_sample.shape[0]
    rows = -(-(bp + bs) // 8) * 8
    c_all = jnp.concatenate([c_prompt, c_sample, jnp.zeros((rows - bp - bs, D_MODEL), F32)], axis=0)
    mod = _mod_call(c_all, ada_w[0].astype(BF16), ada_b[0].reshape(1, 3 * D_MODEL))
    mod3 = mod.reshape(rows, 3, D_MODEL)

    pw = _prep_weights(pre_norm_g[0], w_in[0], a_q_norm_g[0], a_k_norm_g[0], b_q_norm_g[0],
                       b_q_up[0], b_kv_norm_g[0], b_kv_up[0])
    post_g = post_norm_g[0].reshape(1, D_MODEL)
    a_o, b_o, w_oo = a_out[0].astype(BF16), b_out[0].astype(BF16), w_o[0].astype(BF16)

    y_prompt = _layer(x_prompt, mod3[:bp], pw, post_g, a_o, b_o, w_oo)
    y_sample = _layer(x_sample, mod3[bp:bp + bs], pw, post_g, a_o, b_o, w_oo)
    return (y_prompt, y_sample)
```

```python
import functools

import jax
import jax.numpy as jnp
from jax import lax
from jax.experimental import pallas as pl
from jax.experimental.pallas import tpu as pltpu

D_MODEL = 1024
GRID_W = 64
ROPE_THETA = 10000.0
EPS = 1e-6

A_HEADS = 8
A_KV_HEADS = 2
A_HEAD_DIM = 64
A_WIDTH = A_HEADS * A_HEAD_DIM

B_HEADS = 8
B_NOPE = 64
B_ROPE = 32
B_V = 64
B_Q_RANK = 384
B_KV_RANK = 256
B_WIDTH = B_HEADS * B_V

IN_SIZES = (A_WIDTH, A_KV_HEADS * A_HEAD_DIM, A_KV_HEADS * A_HEAD_DIM, A_WIDTH,
            B_Q_RANK, B_KV_RANK, B_ROPE, B_WIDTH, D_MODEL, D_MODEL)

LOG2E = 1.4426950408889634
LANES = 128
QK_PAD = 128
V_ROWS = 64
L_ROWS = 16
PROJ_TILE = 256
ATTN_TQ = 512
ATTN_TK = 512
ATTN_DEPTH = 2
ATTN_UNROLL = 4
SAFE_SCORE_BOUND = 60.0
BOUND_SLACK = 1.0625
VMEM_LIMIT = 56 * 1024 * 1024

F32 = jnp.float32
BF16 = jnp.bfloat16


def _rope_tables_t(n, d_rot):
    rows = n // GRID_W
    row_ids = jnp.repeat(jnp.arange(rows, dtype=F32), GRID_W)
    col_ids = jnp.tile(jnp.arange(GRID_W, dtype=F32), rows)
    d_axis = d_rot // 2
    inv = ROPE_THETA ** (-jnp.arange(0, d_axis, 2, dtype=F32) / d_axis)
    ang = jnp.concatenate([row_ids[:, None] * inv, col_ids[:, None] * inv], axis=-1)
    return jnp.cos(ang).T, jnp.sin(ang).T


def _const_spec(shape):
    nd = len(shape)
    return pl.BlockSpec(shape, lambda *_: (0,) * nd, pipeline_mode=pl.Buffered(1))


def _mod_kernel(c_ref, w_ref, b_ref, o_ref):
    c = c_ref[...]
    sc = (c * jax.nn.sigmoid(c)).astype(BF16)
    o_ref[...] = jnp.dot(sc, w_ref[...], preferred_element_type=F32) + b_ref[...]


def _mod_call(c_pad, ada_w, ada_b):
    rows = c_pad.shape[0]
    return pl.pallas_call(
        _mod_kernel,
        out_shape=jax.ShapeDtypeStruct((rows, 3 * D_MODEL), F32),
        name="adaln_mod",
    )(c_pad, ada_w, ada_b)


def _rope_t(x, cos, sin):
    half = x.shape[0] // 2
    x1, x2 = x[:half], x[half:]
    return jnp.concatenate([x1 * cos - x2 * sin, x2 * cos + x1 * sin], axis=0)


def _rms_t(x, g_col):
    ms = jnp.mean(x * x, axis=0, keepdims=True)
    return x * lax.rsqrt(ms + EPS) * g_col


def _proj_kernel(x_ref, mod_ref, pre_g_ref, wt_ref, wn_ref, aqg_ref, akg_ref, bqg_ref, bkvg_ref,
                 bqup_ref, waug_ref, bvup_ref, cos_a_ref, sin_a_ref, cos_b_ref, sin_b_ref,
                 qa_ref, ka_ref, va_ref, ag_ref, bg_ref, ma_ref, mb_ref, qb_ref, kb_ref, vb_ref,
                 kna_ref, knb_ref):
    t = x_ref.shape[1]
    one_row = (lax.broadcasted_iota(jnp.int32, (8, t), 0) == 0).astype(F32)
    x = x_ref[0]
    shift = mod_ref[0, 0:1, :]
    scale = mod_ref[0, 1:2, :]
    ms = jnp.mean(x * x, axis=-1, keepdims=True)
    h = (x * lax.rsqrt(ms + EPS) * pre_g_ref[...]) * (1.0 + scale) + shift
    hb = h.astype(BF16)

    t1 = lax.dot_general(wt_ref[...], hb, (((1,), (1,)), ((), ())), preferred_element_type=F32)
    cos_a, sin_a = cos_a_ref[...], sin_a_ref[...]
    cos_b, sin_b = cos_b_ref[...], sin_b_ref[...]
    aqg, akg = aqg_ref[...], akg_ref[...]

    q_scale = (A_HEAD_DIM ** -0.5) * LOG2E
    for hd in range(A_HEADS):
        q = t1[hd * A_HEAD_DIM:(hd + 1) * A_HEAD_DIM]
        q = _rope_t(_rms_t(q, aqg), cos_a, sin_a) * q_scale
        qa_ref[0, hd * A_HEAD_DIM:(hd + 1) * A_HEAD_DIM, :] = q.astype(BF16)
    off = A_WIDTH
    for g in range(A_KV_HEADS):
        k = t1[off + g * A_HEAD_DIM:off + (g + 1) * A_HEAD_DIM]
        k = _rope_t(_rms_t(k, akg), cos_a, sin_a)
        slab = jnp.concatenate([k, one_row, jnp.zeros((QK_PAD - A_HEAD_DIM - 8, t), F32)], axis=0)
        ka_ref[0, g] = slab.T.astype(BF16)
        kn = jnp.max(jnp.sum(k * k, axis=0, keepdims=True), axis=1, keepdims=True)
        kna_ref[0, g, 0] = jnp.broadcast_to(kn, (8, LANES))
    off += A_KV_HEADS * A_HEAD_DIM
    for g in range(A_KV_HEADS):
        va_ref[0, g, 0] = t1[off + g * A_HEAD_DIM:off + (g + 1) * A_HEAD_DIM].astype(BF16)
    off += A_KV_HEADS * A_HEAD_DIM

    bqn = _rms_t(t1[off:off + B_Q_RANK], bqg_ref[...]).astype(BF16)
    off += B_Q_RANK
    qb = jnp.dot(bqup_ref[...], bqn, preferred_element_type=F32)
    qb_scale = ((B_NOPE + B_ROPE) ** -0.5) * LOG2E
    for hd in range(B_HEADS):
        blk = qb[hd * QK_PAD:(hd + 1) * QK_PAD]
        roped = _rope_t(blk[B_NOPE:B_NOPE + B_ROPE], cos_b, sin_b)
        full = jnp.concatenate([blk[:B_NOPE], roped, blk[B_NOPE + B_ROPE:]], axis=0) * qb_scale
        qb_ref[0, hd * QK_PAD:(hd + 1) * QK_PAD, :] = full.astype(BF16)

    kr = _rope_t(t1[off:off + B_ROPE], cos_b, sin_b)
    kr_slab = jnp.concatenate([kr, one_row, jnp.zeros((LANES - B_ROPE - 8, t), F32)], axis=0).T

    bkv = jnp.dot(hb, wn_ref[:, 0:B_KV_RANK], preferred_element_type=F32)
    ms = jnp.mean(bkv * bkv, axis=-1, keepdims=True)
    bkvn = bkv * lax.rsqrt(ms + EPS) * bkvg_ref[...]
    lhs = jnp.concatenate([bkvn, kr_slab], axis=1).astype(BF16)
    kb = jnp.dot(lhs, waug_ref[...], preferred_element_type=F32)
    kb_ref[0] = kb.astype(BF16)
    for hd in range(B_HEADS):
        blk = kb[:, hd * QK_PAD:(hd + 1) * QK_PAD]
        ksq = jnp.sum(blk * blk, axis=1, keepdims=True) - 1.0
        knb_ref[0, hd, 0] = jnp.broadcast_to(jnp.max(ksq, axis=0, keepdims=True), (8, LANES))
    vb = lax.dot_general(bvup_ref[...], bkvn.astype(BF16), (((1,), (1,)), ((), ())),
                         preferred_element_type=F32)
    for hd in range(B_HEADS):
        vb_ref[0, hd, 0] = vb[hd * B_V:(hd + 1) * B_V].astype(BF16)

    c0 = B_KV_RANK
    ag = jnp.dot(hb, wn_ref[:, c0:c0 + A_WIDTH], preferred_element_type=F32)
    ag_ref[0] = (ag * jax.nn.sigmoid(ag)).astype(BF16)
    c0 += A_WIDTH
    bg = jnp.dot(hb, wn_ref[:, c0:c0 + B_WIDTH], preferred_element_type=F32)
    bg_ref[0] = (bg * jax.nn.sigmoid(bg)).astype(BF16)
    c0 += B_WIDTH
    ma = jnp.dot(hb, wn_ref[:, c0:c0 + D_MODEL], preferred_element_type=F32)
    ma_ref[0] = jax.nn.sigmoid(ma).astype(BF16)
    c0 += D_MODEL
    mb = jnp.dot(hb, wn_ref[:, c0:c0 + D_MODEL], preferred_element_type=F32)
    mb_ref[0] = jax.nn.sigmoid(mb).astype(BF16)


def _proj_call(x, mod3, pw, tables):
    bsz, n, _ = x.shape
    t = PROJ_TILE
    nt = n // t
    cos_a, sin_a, cos_b, sin_b = tables
    tok = lambda b, i: (b, i, 0)
    feat = lambda b, i: (b, 0, i)
    in_specs = [
        pl.BlockSpec((1, t, D_MODEL), tok),
        pl.BlockSpec((1, 3, D_MODEL), lambda b, i: (b, 0, 0)),
        _const_spec((1, D_MODEL)),
        _const_spec(pw["wt"].shape),
        _const_spec(pw["wn"].shape),
        _const_spec((A_HEAD_DIM, 1)),
        _const_spec((A_HEAD_DIM, 1)),
        _const_spec((B_Q_RANK, 1)),
        _const_spec((1, B_KV_RANK)),
        _const_spec(pw["bqup"].shape),
        _const_spec(pw["waug"].shape),
        _const_spec(pw["bvup"].shape),
        pl.BlockSpec((A_HEAD_DIM // 2, t), lambda b, i: (0, i)),
        pl.BlockSpec((A_HEAD_DIM // 2, t), lambda b, i: (0, i)),
        pl.BlockSpec((B_ROPE // 2, t), lambda b, i: (0, i)),
        pl.BlockSpec((B_ROPE // 2, t), lambda b, i: (0, i)),
    ]
    out_shape = [
        jax.ShapeDtypeStruct((bsz, A_WIDTH, n), BF16),
        jax.ShapeDtypeStruct((bsz, A_KV_HEADS, n, QK_PAD), BF16),
        jax.ShapeDtypeStruct((bsz, A_KV_HEADS, nt, V_ROWS, t), BF16),
        jax.ShapeDtypeStruct((bsz, n, A_WIDTH), BF16),
        jax.ShapeDtypeStruct((bsz, n, B_WIDTH), BF16),
        jax.ShapeDtypeStruct((bsz, n, D_MODEL), BF16),
        jax.ShapeDtypeStruct((bsz, n, D_MODEL), BF16),
        jax.ShapeDtypeStruct((bsz, B_HEADS * QK_PAD, n), BF16),
        jax.ShapeDtypeStruct((bsz, n, B_HEADS * QK_PAD), BF16),
        jax.ShapeDtypeStruct((bsz, B_HEADS, nt, V_ROWS, t), BF16),
        jax.ShapeDtypeStruct((bsz, A_KV_HEADS, nt, 8, LANES), F32),
        jax.ShapeDtypeStruct((bsz, B_HEADS, nt, 8, LANES), F32),
    ]
    out_specs = [
        pl.BlockSpec((1, A_WIDTH, t), feat),
        pl.BlockSpec((1, A_KV_HEADS, t, QK_PAD), lambda b, i: (b, 0, i, 0)),
        pl.BlockSpec((1, A_KV_HEADS, 1, V_ROWS, t), lambda b, i: (b, 0, i, 0, 0)),
        pl.BlockSpec((1, t, A_WIDTH), tok),
        pl.BlockSpec((1, t, B_WIDTH), tok),
        pl.BlockSpec((1, t, D_MODEL), tok),
        pl.BlockSpec((1, t, D_MODEL), tok),
        pl.BlockSpec((1, B_HEADS * QK_PAD, t), feat),
        pl.BlockSpec((1, t, B_HEADS * QK_PAD), tok),
        pl.BlockSpec((1, B_HEADS, 1, V_ROWS, t), lambda b, i: (b, 0, i, 0, 0)),
        pl.BlockSpec((1, A_KV_HEADS, 1, 8, LANES), lambda b, i: (b, 0, i, 0, 0)),
        pl.BlockSpec((1, B_HEADS, 1, 8, LANES), lambda b, i: (b, 0, i, 0, 0)),
    ]
    return pl.pallas_call(
        _proj_kernel,
        grid=(bsz, nt),
        in_specs=in_specs,
        out_specs=out_specs,
        out_shape=out_shape,
        compiler_params=pltpu.CompilerParams(
            dimension_semantics=("arbitrary", "arbitrary"), vmem_limit_bytes=VMEM_LIMIT),
        name="in_proj",
    )(x, mod3, pw["pre_g"], pw["wt"], pw["wn"], pw["aqg"], pw["akg"], pw["bqg"], pw["bkvg"],
      pw["bqup"], pw["waug"], pw["bvup"], cos_a, sin_a, cos_b, sin_b)


def _attn_kernel(q_ref, k_ref, v_ref, kn_ref, o_ref, *s_refs, n_steps, k_4d, q_feat):
    depth = len(s_refs)
    q_in = q_ref[0]
    tq = q_in.shape[1]
    q = q_in[:q_feat]
    sub = ATTN_TK // PROJ_TILE
    ones = jnp.ones((L_ROWS, ATTN_TK), BF16)
    tail_rows = QK_PAD - q_feat - L_ROWS

    qf = q.astype(F32)
    kmax2 = jnp.max(jnp.max(kn_ref[0, 0], axis=0), axis=1, keepdims=True)[0:1]
    bound = jnp.sqrt(jnp.sum(qf * qf, axis=0, keepdims=True) * kmax2) * BOUND_SLACK
    worst = jnp.max(bound)

    def with_shift_row(row):
        first = lax.broadcasted_iota(jnp.int32, (L_ROWS, tq), 0) == 0
        rows = jnp.where(first, row, 0.0).astype(BF16)
        parts = [q, rows]
        if tail_rows:
            parts.append(jnp.zeros((tail_rows, tq), BF16))
        return jnp.concatenate(parts, axis=0)

    def key_chunk(j):
        off = pl.multiple_of(j * ATTN_TK, ATTN_TK)
        if k_4d:
            return k_ref[0, 0, pl.ds(off, ATTN_TK), :]
        return k_ref[0, pl.ds(off, ATTN_TK), :]

    def value_chunk(j):
        v = jnp.concatenate([v_ref[0, 0, j * sub + c] for c in range(sub)], axis=1)
        return jnp.concatenate([v, ones], axis=0)

    def finish(acc):
        o_ref[0] = acc[:V_ROWS] / acc[V_ROWS:V_ROWS + 1]

    acc0 = jnp.zeros((V_ROWS + L_ROWS, tq), F32)

    @pl.when(worst <= SAFE_SCORE_BOUND)
    def _():
        q_aug = with_shift_row(-bound)

        def scores(j, s_ref):
            s_ref[...] = jnp.dot(key_chunk(j), q_aug, preferred_element_type=F32)

        def stages(j0, acc, count, score_last):
            for u in range(count):
                j = j0 + u
                if u + 1 < count or score_last:
                    scores(j + 1, s_refs[(u + 1) % 2])
                p = jnp.exp2(s_refs[u % 2][...]).astype(BF16)
                acc = acc + jnp.dot(value_chunk(j), p, preferred_element_type=F32)
            return acc

        scores(0, s_refs[0])
        trips = n_steps // ATTN_UNROLL - 1
        acc = lax.fori_loop(
            0, trips, lambda jj, acc: stages(jj * ATTN_UNROLL, acc, ATTN_UNROLL, True), acc0)
        finish(stages(trips * ATTN_UNROLL, acc, ATTN_UNROLL, False))

    @pl.when(jnp.logical_not(worst <= SAFE_SCORE_BOUND))
    def _():
        q_pad = with_shift_row(jnp.zeros((1, tq), F32))

        def scores(j, s_ref):
            s = jnp.dot(key_chunk(j), q_pad, preferred_element_type=F32)
            s_ref[...] = s
            return jnp.max(s, axis=0, keepdims=True)

        def accumulate(j, s_ref, m, cmax, acc):
            m_new = jnp.maximum(m, cmax)
            alpha = jnp.exp2(m - m_new)
            p = jnp.exp2(s_ref[...] - m_new).astype(BF16)
            return m_new, alpha * acc + jnp.dot(value_chunk(j), p, preferred_element_type=F32)

        def body(jj, carry):
            m, cmax, acc = carry
            for u in range(depth):
                j = jj * depth + u
                cmax_next = scores(jnp.minimum(j + 1, n_steps - 1), s_refs[(u + 1) % depth])
                m, acc = accumulate(j, s_refs[u], m, cmax, acc)
                cmax = cmax_next
            return m, cmax, acc

        m0 = jnp.full((1, tq), -1e30, F32)
        cmax0 = scores(0, s_refs[0])
        _, _, acc = lax.fori_loop(0, n_steps // depth, body, (m0, cmax0, acc0))
        finish(acc)


def _attn_call(q, k, v, kn, *, heads, q_rows, q_feat, group, k_4d, name):
    bsz, _, n = q.shape
    tq = ATTN_TQ
    n_chunks = v.shape[2]
    if k_4d:
        k_spec = pl.BlockSpec((1, 1, n, QK_PAD), lambda b, h, i: (b, h // group, 0, 0))
    else:
        k_spec = pl.BlockSpec((1, n, QK_PAD), lambda b, h, i: (b, 0, h))
    n_steps = n // ATTN_TK
    assert n_steps % ATTN_DEPTH == 0 and n_steps % ATTN_UNROLL == 0
    kernel = functools.partial(_attn_kernel, n_steps=n_steps, k_4d=k_4d, q_feat=q_feat)
    return pl.pallas_call(
        kernel,
        grid=(bsz, heads, n // tq),
        in_specs=[
            pl.BlockSpec((1, q_rows, tq), lambda b, h, i: (b, h, i)),
            k_spec,
            pl.BlockSpec((1, 1, n_chunks, V_ROWS, PROJ_TILE), lambda b, h, i: (b, h // group, 0, 0, 0)),
            pl.BlockSpec((1, 1, n_chunks, 8, LANES), lambda b, h, i: (b, h // group, 0, 0, 0)),
        ],
        out_specs=pl.BlockSpec((1, V_ROWS, tq), lambda b, h, i: (b, h, i)),
        out_shape=jax.ShapeDtypeStruct((bsz, heads * V_ROWS, n), F32),
        scratch_shapes=[pltpu.VMEM((ATTN_TK, tq), F32)] * ATTN_DEPTH,
        compiler_params=pltpu.CompilerParams(
            dimension_semantics=("arbitrary", "arbitrary", "arbitrary"), vmem_limit_bytes=VMEM_LIMIT),
        name=name,
    )(q, k, v, kn)


def _out_kernel(x_ref, mod_ref, post_g_ref, ya_ref, yb_ref, ag_ref, bg_ref, ma_ref, mb_ref,
                aout_ref, bout_ref, wo_ref, y_ref):
    ga = (ya_ref[0].T * ag_ref[0].astype(F32)).astype(BF16)
    gb = (yb_ref[0].T * bg_ref[0].astype(F32)).astype(BF16)
    pa = jnp.dot(ga, aout_ref[...], preferred_element_type=F32)
    pb = jnp.dot(gb, bout_ref[...], preferred_element_type=F32)
    merged = ma_ref[0].astype(F32) * pa + mb_ref[0].astype(F32) * pb
    z = jnp.dot(merged.astype(BF16), wo_ref[...], preferred_element_type=F32)
    ms = jnp.mean(z * z, axis=-1, keepdims=True)
    zn = z * lax.rsqrt(ms + EPS) * post_g_ref[...]
    y_ref[0] = x_ref[0] + mod_ref[0, 2:3, :] * zn


def _out_call(x, mod3, post_g, ya, yb, ag, bg, ma, mb, a_out, b_out, w_o):
    bsz, n, _ = x.shape
    t = PROJ_TILE
    tok = lambda b, i: (b, i, 0)
    feat = lambda b, i: (b, 0, i)
    return pl.pallas_call(
        _out_kernel,
        grid=(bsz, n // t),
        in_specs=[
            pl.BlockSpec((1, t, D_MODEL), tok),
            pl.BlockSpec((1, 3, D_MODEL), lambda b, i: (b, 0, 0)),
            _const_spec((1, D_MODEL)),
            pl.BlockSpec((1, A_WIDTH, t), feat),
            pl.BlockSpec((1, B_WIDTH, t), feat),
            pl.BlockSpec((1, t, A_WIDTH), tok),
            pl.BlockSpec((1, t, B_WIDTH), tok),
            pl.BlockSpec((1, t, D_MODEL), tok),
            pl.BlockSpec((1, t, D_MODEL), tok),
            _const_spec(a_out.shape),
            _const_spec(b_out.shape),
            _const_spec(w_o.shape),
        ],
        out_specs=pl.BlockSpec((1, t, D_MODEL), tok),
        out_shape=jax.ShapeDtypeStruct((bsz, n, D_MODEL), F32),
        compiler_params=pltpu.CompilerParams(
            dimension_semantics=("arbitrary", "arbitrary"), vmem_limit_bytes=VMEM_LIMIT),
        name="out_proj",
    )(x, mod3, post_g, ya, yb, ag, bg, ma, mb, a_out, b_out, w_o)


def _prep_weights(pre_g, w_in, a_q_g, a_k_g, b_q_g, b_q_up, b_kv_g, b_kv_up):
    pts = [0]
    for s in IN_SIZES:
        pts.append(pts[-1] + s)
    seg = lambda i: w_in[:, pts[i]:pts[i + 1]]
    aq, ak, av, ag, bq, bkv, bkr, bg, ma, mb = (seg(i) for i in range(10))
    wt = jnp.concatenate([aq, ak, av, bq, bkr], axis=1).T.astype(BF16)
    wn = jnp.concatenate([bkv, ag, bg, ma, mb], axis=1).astype(BF16)

    qd = B_NOPE + B_ROPE
    bqup = b_q_up.reshape(B_Q_RANK, B_HEADS, qd)
    bqup = jnp.pad(bqup, ((0, 0), (0, 0), (0, QK_PAD - qd))).reshape(B_Q_RANK, B_HEADS * QK_PAD)

    kvup = b_kv_up.reshape(B_KV_RANK, B_HEADS, B_NOPE + B_V)
    k_part = jnp.pad(kvup[:, :, :B_NOPE], ((0, 0), (0, 0), (0, QK_PAD - B_NOPE)))
    eye = jnp.eye(LANES, QK_PAD, k=B_NOPE, dtype=F32)[:, None, :]
    eye = jnp.where(jnp.arange(LANES)[:, None, None] <= B_ROPE, eye, 0.0)
    r_part = jnp.broadcast_to(eye, (LANES, B_HEADS, QK_PAD))
    waug = jnp.concatenate([k_part, r_part], axis=0).reshape(B_KV_RANK + LANES, B_HEADS * QK_PAD)
    bvup = kvup[:, :, B_NOPE:].reshape(B_KV_RANK, B_HEADS * B_V).T

    return {
        "pre_g": pre_g.reshape(1, D_MODEL),
        "wt": wt, "wn": wn,
        "aqg": a_q_g.reshape(A_HEAD_DIM, 1), "akg": a_k_g.reshape(A_HEAD_DIM, 1),
        "bqg": b_q_g.reshape(B_Q_RANK, 1), "bkvg": b_kv_g.reshape(1, B_KV_RANK),
        "bqup": bqup.T.astype(BF16), "waug": waug.astype(BF16), "bvup": bvup.astype(BF16),
    }


def _layer(x, mod3, pw, post_g, a_out, b_out, w_o):
    n = x.shape[1]
    tables = _rope_tables_t(n, A_HEAD_DIM) + _rope_tables_t(n, B_ROPE)
    qa, ka, va, ag, bg, ma, mb, qb, kb, vb, kna, knb = _proj_call(x, mod3, pw, tables)
    ya = _attn_call(qa, ka, va, kna, heads=A_HEADS, q_rows=A_HEAD_DIM, q_feat=A_HEAD_DIM,
                    group=A_HEADS // A_KV_HEADS, k_4d=True, name="attn_gqa")
    yb = _attn_call(qb, kb, vb, knb, heads=B_HEADS, q_rows=QK_PAD, q_feat=B_NOPE + B_ROPE,
                    group=1, k_4d=False, name="attn_mla")
    return _out_call(x, mod3, post_g, ya, yb, ag, bg, ma, mb, a_out, b_out, w_o)


def kernel(x_prompt, x_sample, c_prompt, c_sample, ada_w, ada_b, pre_norm_g, post_norm_g, w_in,
           a_q_norm_g, a_k_norm_g, b_q_norm_g, b_q_up, b_kv_norm_g, b_kv_up, a_out, b_out, w_o):
    assert ada_w.shape[0] == 1, "single layer"
    bp, bs = c_prompt.shape[0], c_sample.shape[0]
    rows = -(-(bp + bs) // 8) * 8
    c_all = jnp.concatenate([c_prompt, c_sample, jnp.zeros((rows - bp - bs, D_MODEL), F32)], axis=0)
    mod = _mod_call(c_all, ada_w[0].astype(BF16), ada_b[0].reshape(1, 3 * D_MODEL))
    mod3 = mod.reshape(rows, 3, D_MODEL)

    pw = _prep_weights(pre_norm_g[0], w_in[0], a_q_norm_g[0], a_k_norm_g[0], b_q_norm_g[0],
                       b_q_up[0], b_kv_norm_g[0], b_kv_up[0])
    post_g = post_norm_g[0].reshape(1, D_MODEL)
    a_o, b_o, w_oo = a_out[0].astype(BF16), b_out[0].astype(BF16), w_o[0].astype(BF16)

    y_prompt = _layer(x_prompt, mod3[:bp], pw, post_g, a_o, b_o, w_oo)
    y_sample = _layer(x_sample, mod3[bp:bp + bs], pw, post_g, a_o, b_o, w_oo)
    return (y_prompt, y_sample)
```

```python
import functools

import jax
import jax.numpy as jnp
from jax import lax
from jax.experimental import pallas as pl
from jax.experimental.pallas import tpu as pltpu

D_MODEL = 1024
GRID_W = 64
ROPE_THETA = 10000.0
EPS = 1e-6

A_HEADS = 8
A_KV_HEADS = 2
A_HEAD_DIM = 64
A_WIDTH = A_HEADS * A_HEAD_DIM

B_HEADS = 8
B_NOPE = 64
B_ROPE = 32
B_V = 64
B_Q_RANK = 384
B_KV_RANK = 256
B_WIDTH = B_HEADS * B_V

IN_SIZES = (A_WIDTH, A_KV_HEADS * A_HEAD_DIM, A_KV_HEADS * A_HEAD_DIM, A_WIDTH,
            B_Q_RANK, B_KV_RANK, B_ROPE, B_WIDTH, D_MODEL, D_MODEL)

LOG2E = 1.4426950408889634
LANES = 128
QK_PAD = 128
V_ROWS = 64
L_ROWS = 16
PROJ_TILE = 256
ATTN_TQ = 512
ATTN_TK = 512
ATTN_DEPTH = 2
ATTN_UNROLL = 8
SAFE_SCORE_BOUND = 60.0
BOUND_SLACK = 1.0625
VMEM_LIMIT = 56 * 1024 * 1024

F32 = jnp.float32
BF16 = jnp.bfloat16


def _rope_tables_t(n, d_rot):
    rows = n // GRID_W
    row_ids = jnp.repeat(jnp.arange(rows, dtype=F32), GRID_W)
    col_ids = jnp.tile(jnp.arange(GRID_W, dtype=F32), rows)
    d_axis = d_rot // 2
    inv = ROPE_THETA ** (-jnp.arange(0, d_axis, 2, dtype=F32) / d_axis)
    ang = jnp.concatenate([row_ids[:, None] * inv, col_ids[:, None] * inv], axis=-1)
    return jnp.cos(ang).T, jnp.sin(ang).T


def _const_spec(shape):
    nd = len(shape)
    return pl.BlockSpec(shape, lambda *_: (0,) * nd, pipeline_mode=pl.Buffered(1))


def _mod_kernel(c_ref, w_ref, b_ref, o_ref):
    c = c_ref[...]
    sc = (c * jax.nn.sigmoid(c)).astype(BF16)
    o_ref[...] = jnp.dot(sc, w_ref[...], preferred_element_type=F32) + b_ref[...]


def _mod_call(c_pad, ada_w, ada_b):
    rows = c_pad.shape[0]
    return pl.pallas_call(
        _mod_kernel,
        out_shape=jax.ShapeDtypeStruct((rows, 3 * D_MODEL), F32),
        name="adaln_mod",
    )(c_pad, ada_w, ada_b)


def _rope_t(x, cos, sin):
    half = x.shape[0] // 2
    x1, x2 = x[:half], x[half:]
    return jnp.concatenate([x1 * cos - x2 * sin, x2 * cos + x1 * sin], axis=0)


def _rms_t(x, g_col):
    ms = jnp.mean(x * x, axis=0, keepdims=True)
    return x * lax.rsqrt(ms + EPS) * g_col


def _proj_kernel(x_ref, mod_ref, pre_g_ref, wt_ref, wn_ref, aqg_ref, akg_ref, bqg_ref, bkvg_ref,
                 bqup_ref, waug_ref, bvup_ref, cos_a_ref, sin_a_ref, cos_b_ref, sin_b_ref,
                 qa_ref, ka_ref, va_ref, ag_ref, bg_ref, ma_ref, mb_ref, qb_ref, kb_ref, vb_ref,
                 kna_ref, knb_ref, qna_ref, qnb_ref):
    t = x_ref.shape[1]

    def tile_max_sq(x_t):
        sq = jnp.sum(x_t * x_t, axis=0, keepdims=True)
        return jnp.broadcast_to(jnp.max(sq, axis=1, keepdims=True), (8, LANES))

    one_row = (lax.broadcasted_iota(jnp.int32, (8, t), 0) == 0).astype(F32)
    x = x_ref[0]
    shift = mod_ref[0, 0:1, :]
    scale = mod_ref[0, 1:2, :]
    ms = jnp.mean(x * x, axis=-1, keepdims=True)
    h = (x * lax.rsqrt(ms + EPS) * pre_g_ref[...]) * (1.0 + scale) + shift
    hb = h.astype(BF16)

    t1 = lax.dot_general(wt_ref[...], hb, (((1,), (1,)), ((), ())), preferred_element_type=F32)
    cos_a, sin_a = cos_a_ref[...], sin_a_ref[...]
    cos_b, sin_b = cos_b_ref[...], sin_b_ref[...]
    aqg, akg = aqg_ref[...], akg_ref[...]

    q_scale = (A_HEAD_DIM ** -0.5) * LOG2E
    for hd in range(A_HEADS):
        q = t1[hd * A_HEAD_DIM:(hd + 1) * A_HEAD_DIM]
        q = _rope_t(_rms_t(q, aqg), cos_a, sin_a) * q_scale
        qa_ref[0, hd * A_HEAD_DIM:(hd + 1) * A_HEAD_DIM, :] = q.astype(BF16)
        qna_ref[0, hd, 0] = tile_max_sq(q)
    off = A_WIDTH
    for g in range(A_KV_HEADS):
        k = t1[off + g * A_HEAD_DIM:off + (g + 1) * A_HEAD_DIM]
        k = _rope_t(_rms_t(k, akg), cos_a, sin_a)
        slab = jnp.concatenate([k, one_row, jnp.zeros((QK_PAD - A_HEAD_DIM - 8, t), F32)], axis=0)
        ka_ref[0, g] = slab.T.astype(BF16)
        kna_ref[0, g, 0] = tile_max_sq(k)
    off += A_KV_HEADS * A_HEAD_DIM
    for g in range(A_KV_HEADS):
        va_ref[0, g, 0] = t1[off + g * A_HEAD_DIM:off + (g + 1) * A_HEAD_DIM].astype(BF16)
    off += A_KV_HEADS * A_HEAD_DIM

    bqn = _rms_t(t1[off:off + B_Q_RANK], bqg_ref[...]).astype(BF16)
    off += B_Q_RANK
    qb = jnp.dot(bqup_ref[...], bqn, preferred_element_type=F32)
    qb_scale = ((B_NOPE + B_ROPE) ** -0.5) * LOG2E
    for hd in range(B_HEADS):
        blk = qb[hd * QK_PAD:(hd + 1) * QK_PAD]
        roped = _rope_t(blk[B_NOPE:B_NOPE + B_ROPE], cos_b, sin_b)
        full = jnp.concatenate([blk[:B_NOPE], roped, blk[B_NOPE + B_ROPE:]], axis=0) * qb_scale
        qb_ref[0, hd * QK_PAD:(hd + 1) * QK_PAD, :] = full.astype(BF16)
        qnb_ref[0, hd, 0] = tile_max_sq(full)

    kr = _rope_t(t1[off:off + B_ROPE], cos_b, sin_b)
    kr_slab = jnp.concatenate([kr, one_row, jnp.zeros((LANES - B_ROPE - 8, t), F32)], axis=0).T

    bkv = jnp.dot(hb, wn_ref[:, 0:B_KV_RANK], preferred_element_type=F32)
    ms = jnp.mean(bkv * bkv, axis=-1, keepdims=True)
    bkvn = bkv * lax.rsqrt(ms + EPS) * bkvg_ref[...]
    lhs = jnp.concatenate([bkvn, kr_slab], axis=1).astype(BF16)
    kb = jnp.dot(lhs, waug_ref[...], preferred_element_type=F32)
    kb_ref[0] = kb.astype(BF16)
    for hd in range(B_HEADS):
        blk = kb[:, hd * QK_PAD:(hd + 1) * QK_PAD]
        ksq = jnp.sum(blk * blk, axis=1, keepdims=True) - 1.0
        knb_ref[0, hd, 0] = jnp.broadcast_to(jnp.max(ksq, axis=0, keepdims=True), (8, LANES))
    vb = lax.dot_general(bvup_ref[...], bkvn.astype(BF16), (((1,), (1,)), ((), ())),
                         preferred_element_type=F32)
    for hd in range(B_HEADS):
        vb_ref[0, hd, 0] = vb[hd * B_V:(hd + 1) * B_V].astype(BF16)

    c0 = B_KV_RANK
    ag = jnp.dot(hb, wn_ref[:, c0:c0 + A_WIDTH], preferred_element_type=F32)
    ag_ref[0] = (ag * jax.nn.sigmoid(ag)).astype(BF16)
    c0 += A_WIDTH
    bg = jnp.dot(hb, wn_ref[:, c0:c0 + B_WIDTH], preferred_element_type=F32)
    bg_ref[0] = (bg * jax.nn.sigmoid(bg)).astype(BF16)
    c0 += B_WIDTH
    ma = jnp.dot(hb, wn_ref[:, c0:c0 + D_MODEL], preferred_element_type=F32)
    ma_ref[0] = jax.nn.sigmoid(ma).astype(BF16)
    c0 += D_MODEL
    mb = jnp.dot(hb, wn_ref[:, c0:c0 + D_MODEL], preferred_element_type=F32)
    mb_ref[0] = jax.nn.sigmoid(mb).astype(BF16)


def _proj_call(x, mod3, pw, tables):
    bsz, n, _ = x.shape
    t = PROJ_TILE
    nt = n // t
    cos_a, sin_a, cos_b, sin_b = tables
    tok = lambda b, i: (b, i, 0)
    feat = lambda b, i: (b, 0, i)
    in_specs = [
        pl.BlockSpec((1, t, D_MODEL), tok),
        pl.BlockSpec((1, 3, D_MODEL), lambda b, i: (b, 0, 0)),
        _const_spec((1, D_MODEL)),
        _const_spec(pw["wt"].shape),
        _const_spec(pw["wn"].shape),
        _const_spec((A_HEAD_DIM, 1)),
        _const_spec((A_HEAD_DIM, 1)),
        _const_spec((B_Q_RANK, 1)),
        _const_spec((1, B_KV_RANK)),
        _const_spec(pw["bqup"].shape),
        _const_spec(pw["waug"].shape),
        _const_spec(pw["bvup"].shape),
        pl.BlockSpec((A_HEAD_DIM // 2, t), lambda b, i: (0, i)),
        pl.BlockSpec((A_HEAD_DIM // 2, t), lambda b, i: (0, i)),
        pl.BlockSpec((B_ROPE // 2, t), lambda b, i: (0, i)),
        pl.BlockSpec((B_ROPE // 2, t), lambda b, i: (0, i)),
    ]
    out_shape = [
        jax.ShapeDtypeStruct((bsz, A_WIDTH, n), BF16),
        jax.ShapeDtypeStruct((bsz, A_KV_HEADS, n, QK_PAD), BF16),
        jax.ShapeDtypeStruct((bsz, A_KV_HEADS, nt, V_ROWS, t), BF16),
        jax.ShapeDtypeStruct((bsz, n, A_WIDTH), BF16),
        jax.ShapeDtypeStruct((bsz, n, B_WIDTH), BF16),
        jax.ShapeDtypeStruct((bsz, n, D_MODEL), BF16),
        jax.ShapeDtypeStruct((bsz, n, D_MODEL), BF16),
        jax.ShapeDtypeStruct((bsz, B_HEADS * QK_PAD, n), BF16),
        jax.ShapeDtypeStruct((bsz, n, B_HEADS * QK_PAD), BF16),
        jax.ShapeDtypeStruct((bsz, B_HEADS, nt, V_ROWS, t), BF16),
        jax.ShapeDtypeStruct((bsz, A_KV_HEADS, nt, 8, LANES), F32),
        jax.ShapeDtypeStruct((bsz, B_HEADS, nt, 8, LANES), F32),
        jax.ShapeDtypeStruct((bsz, A_HEADS, nt, 8, LANES), F32),
        jax.ShapeDtypeStruct((bsz, B_HEADS, nt, 8, LANES), F32),
    ]
    out_specs = [
        pl.BlockSpec((1, A_WIDTH, t), feat),
        pl.BlockSpec((1, A_KV_HEADS, t, QK_PAD), lambda b, i: (b, 0, i, 0)),
        pl.BlockSpec((1, A_KV_HEADS, 1, V_ROWS, t), lambda b, i: (b, 0, i, 0, 0)),
        pl.BlockSpec((1, t, A_WIDTH), tok),
        pl.BlockSpec((1, t, B_WIDTH), tok),
        pl.BlockSpec((1, t, D_MODEL), tok),
        pl.BlockSpec((1, t, D_MODEL), tok),
        pl.BlockSpec((1, B_HEADS * QK_PAD, t), feat),
        pl.BlockSpec((1, t, B_HEADS * QK_PAD), tok),
        pl.BlockSpec((1, B_HEADS, 1, V_ROWS, t), lambda b, i: (b, 0, i, 0, 0)),
        pl.BlockSpec((1, A_KV_HEADS, 1, 8, LANES), lambda b, i: (b, 0, i, 0, 0)),
        pl.BlockSpec((1, B_HEADS, 1, 8, LANES), lambda b, i: (b, 0, i, 0, 0)),
        pl.BlockSpec((1, A_HEADS, 1, 8, LANES), lambda b, i: (b, 0, i, 0, 0)),
        pl.BlockSpec((1, B_HEADS, 1, 8, LANES), lambda b, i: (b, 0, i, 0, 0)),
    ]
    return pl.pallas_call(
        _proj_kernel,
        grid=(bsz, nt),
        in_specs=in_specs,
        out_specs=out_specs,
        out_shape=out_shape,
        compiler_params=pltpu.CompilerParams(
            dimension_semantics=("arbitrary", "arbitrary"), vmem_limit_bytes=VMEM_LIMIT),
        name="in_proj",
    )(x, mod3, pw["pre_g"], pw["wt"], pw["wn"], pw["aqg"], pw["akg"], pw["bqg"], pw["bkvg"],
      pw["bqup"], pw["waug"], pw["bvup"], cos_a, sin_a, cos_b, sin_b)


def _attn_kernel(small_ref, q_ref, k_ref, v_ref, kn_ref, o_ref, s0_ref, s1_ref, p0_ref, p1_ref, *,
                 n_steps, k_4d, q_feat, heads, three_stage):
    s_refs, p_refs = (s0_ref, s1_ref), (p0_ref, p1_ref)
    depth = len(s_refs)
    q_in = q_ref[0]
    tq = q_in.shape[1]
    q = q_in[:q_feat]
    sub = ATTN_TK // PROJ_TILE
    ones = jnp.ones((L_ROWS, ATTN_TK), BF16)
    tail_rows = QK_PAD - q_feat - L_ROWS
    scores_are_small = small_ref[pl.program_id(0) * heads + pl.program_id(1)] == 1

    qf = q.astype(F32)
    kmax2 = kn_ref[0, 0, 0:1, 0:1]
    bound = jnp.sqrt(jnp.sum(qf * qf, axis=0, keepdims=True) * kmax2) * BOUND_SLACK

    def with_shift_row(row):
        first = lax.broadcasted_iota(jnp.int32, (L_ROWS, tq), 0) == 0
        rows = jnp.where(first, row, 0.0).astype(BF16)
        parts = [q, rows]
        if tail_rows:
            parts.append(jnp.zeros((tail_rows, tq), BF16))
        return jnp.concatenate(parts, axis=0)

    def key_chunk(j):
        off = pl.multiple_of(j * ATTN_TK, ATTN_TK)
        if k_4d:
            return k_ref[0, 0, pl.ds(off, ATTN_TK), :]
        return k_ref[0, pl.ds(off, ATTN_TK), :]

    def value_chunk(j):
        v = jnp.concatenate([v_ref[0, 0, j * sub + c] for c in range(sub)], axis=1)
        return jnp.concatenate([v, ones], axis=0)

    def finish(acc):
        o_ref[0] = acc[:V_ROWS] / acc[V_ROWS:V_ROWS + 1]

    acc0 = jnp.zeros((V_ROWS + L_ROWS, tq), F32)

    @pl.when(scores_are_small)
    def _():
        q_aug = with_shift_row(-bound)

        def scores(j, par):
            s_refs[par % 2][...] = jnp.dot(key_chunk(j), q_aug, preferred_element_type=F32)

        def probabilities(par):
            return jnp.exp2(s_refs[par % 2][...]).astype(BF16)

        def accumulate(j, p, acc):
            return acc + jnp.dot(value_chunk(j), p, preferred_element_type=F32)

        if three_stage:
            def stages(j0, par0, acc, count, score_last):
                for u in range(count):
                    j, par = j0 + u, par0 + u
                    acc = accumulate(j - 1, p_refs[(par - 1) % 2][...], acc)
                    if u + 1 < count or score_last:
                        scores(j + 1, par + 1)
                    p_refs[par % 2][...] = probabilities(par)
                return acc

            scores(0, 0)
            p_refs[0][...] = probabilities(0)
            scores(1, 1)
            first = 1
        else:
            def stages(j0, par0, acc, count, score_last):
                for u in range(count):
                    j, par = j0 + u, par0 + u
                    if u + 1 < count or score_last:
                        scores(j + 1, par + 1)
                    acc = accumulate(j, probabilities(par), acc)
                return acc

            scores(0, 0)
            first = 0

        trips = (n_steps - first - 1) // ATTN_UNROLL
        acc = lax.fori_loop(
            0, trips, lambda jj, acc: stages(first + jj * ATTN_UNROLL, first, acc, ATTN_UNROLL, True),
            acc0)
        done = first + trips * ATTN_UNROLL
        acc = stages(done, done, acc, n_steps - done, False)
        if three_stage:
            acc = accumulate(n_steps - 1, p_refs[(n_steps - 1) % 2][...], acc)
        finish(acc)

    @pl.when(jnp.logical_not(scores_are_small))
    def _():
        q_pad = with_shift_row(jnp.zeros((1, tq), F32))

        def scores(j, s_ref):
            s = jnp.dot(key_chunk(j), q_pad, preferred_element_type=F32)
            s_ref[...] = s
            return jnp.max(s, axis=0, keepdims=True)

        def accumulate(j, s_ref, m, cmax, acc):
            m_new = jnp.maximum(m, cmax)
            alpha = jnp.exp2(m - m_new)
            p = jnp.exp2(s_ref[...] - m_new).astype(BF16)
            return m_new, alpha * acc + jnp.dot(value_chunk(j), p, preferred_element_type=F32)

        def body(jj, carry):
            m, cmax, acc = carry
            for u in range(depth):
                j = jj * depth + u
                cmax_next = scores(jnp.minimum(j + 1, n_steps - 1), s_refs[(u + 1) % depth])
                m, acc = accumulate(j, s_refs[u], m, cmax, acc)
                cmax = cmax_next
            return m, cmax, acc

        m0 = jnp.full((1, tq), -1e30, F32)
        cmax0 = scores(0, s_refs[0])
        _, _, acc = lax.fori_loop(0, n_steps // depth, body, (m0, cmax0, acc0))
        finish(acc)


def _attn_call(q, k, v, qn_tiles, kn_tiles, *, heads, q_rows, q_feat, group, k_4d, three_stage, name):
    bsz, _, n = q.shape
    tq = ATTN_TQ
    n_chunks = v.shape[2]
    kn = jnp.max(kn_tiles, axis=2)
    qmax2 = jnp.max(qn_tiles, axis=(2, 3, 4))
    kmax2 = jnp.repeat(jnp.max(kn, axis=(2, 3)), group, axis=1)
    small = (qmax2 * kmax2 * BOUND_SLACK ** 2 <= SAFE_SCORE_BOUND ** 2).astype(jnp.int32).reshape(-1)
    if k_4d:
        k_spec = pl.BlockSpec((1, 1, n, QK_PAD), lambda b, h, i, _: (b, h // group, 0, 0))
    else:
        k_spec = pl.BlockSpec((1, n, QK_PAD), lambda b, h, i, _: (b, 0, h))
    n_steps = n // ATTN_TK
    assert n_steps >= 2 and n_steps % ATTN_DEPTH == 0 and ATTN_UNROLL % 2 == 0
    kernel = functools.partial(_attn_kernel, n_steps=n_steps, k_4d=k_4d, q_feat=q_feat, heads=heads,
                               three_stage=three_stage)
    grid_spec = pltpu.PrefetchScalarGridSpec(
        num_scalar_prefetch=1,
        grid=(bsz, heads, n // tq),
        in_specs=[
            pl.BlockSpec((1, q_rows, tq), lambda b, h, i, _: (b, h, i)),
            k_spec,
            pl.BlockSpec((1, 1, n_chunks, V_ROWS, PROJ_TILE), lambda b, h, i, _: (b, h // group, 0, 0, 0)),
            pl.BlockSpec((1, 1, 8, LANES), lambda b, h, i, _: (b, h // group, 0, 0)),
        ],
        out_specs=pl.BlockSpec((1, V_ROWS, tq), lambda b, h, i, _: (b, h, i)),
        scratch_shapes=[pltpu.VMEM((ATTN_TK, tq), F32)] * 2 + [pltpu.VMEM((ATTN_TK, tq), BF16)] * 2,
    )
    return pl.pallas_call(
        kernel,
        grid_spec=grid_spec,
        out_shape=jax.ShapeDtypeStruct((bsz, heads * V_ROWS, n), F32),
        compiler_params=pltpu.CompilerParams(
            dimension_semantics=("arbitrary", "arbitrary", "arbitrary"), vmem_limit_bytes=VMEM_LIMIT),
        name=name,
    )(small, q, k, v, kn)


def _out_kernel(x_ref, mod_ref, post_g_ref, ya_ref, yb_ref, ag_ref, bg_ref, ma_ref, mb_ref,
                aout_ref, bout_ref, wo_ref, y_ref):
    ga = (ya_ref[0].T * ag_ref[0].astype(F32)).astype(BF16)
    gb = (yb_ref[0].T * bg_ref[0].astype(F32)).astype(BF16)
    pa = jnp.dot(ga, aout_ref[...], preferred_element_type=F32)
    pb = jnp.dot(gb, bout_ref[...], preferred_element_type=F32)
    merged = ma_ref[0].astype(F32) * pa + mb_ref[0].astype(F32) * pb
    z = jnp.dot(merged.astype(BF16), wo_ref[...], preferred_element_type=F32)
    ms = jnp.mean(z * z, axis=-1, keepdims=True)
    zn = z * lax.rsqrt(ms + EPS) * post_g_ref[...]
    y_ref[0] = x_ref[0] + mod_ref[0, 2:3, :] * zn


def _out_call(x, mod3, post_g, ya, yb, ag, bg, ma, mb, a_out, b_out, w_o):
    bsz, n, _ = x.shape
    t = PROJ_TILE
    tok = lambda b, i: (b, i, 0)
    feat = lambda b, i: (b, 0, i)
    return pl.pallas_call(
        _out_kernel,
        grid=(bsz, n // t),
        in_specs=[
            pl.BlockSpec((1, t, D_MODEL), tok),
            pl.BlockSpec((1, 3, D_MODEL), lambda b, i: (b, 0, 0)),
            _const_spec((1, D_MODEL)),
            pl.BlockSpec((1, A_WIDTH, t), feat),
            pl.BlockSpec((1, B_WIDTH, t), feat),
            pl.BlockSpec((1, t, A_WIDTH), tok),
            pl.BlockSpec((1, t, B_WIDTH), tok),
            pl.BlockSpec((1, t, D_MODEL), tok),
            pl.BlockSpec((1, t, D_MODEL), tok),
            _const_spec(a_out.shape),
            _const_spec(b_out.shape),
            _const_spec(w_o.shape),
        ],
        out_specs=pl.BlockSpec((1, t, D_MODEL), tok),
        out_shape=jax.ShapeDtypeStruct((bsz, n, D_MODEL), F32),
        compiler_params=pltpu.CompilerParams(
            dimension_semantics=("arbitrary", "arbitrary"), vmem_limit_bytes=VMEM_LIMIT),
        name="out_proj",
    )(x, mod3, post_g, ya, yb, ag, bg, ma, mb, a_out, b_out, w_o)


def _prep_weights(pre_g, w_in, a_q_g, a_k_g, b_q_g, b_q_up, b_kv_g, b_kv_up):
    pts = [0]
    for s in IN_SIZES:
        pts.append(pts[-1] + s)
    seg = lambda i: w_in[:, pts[i]:pts[i + 1]]
    aq, ak, av, ag, bq, bkv, bkr, bg, ma, mb = (seg(i) for i in range(10))
    wt = jnp.concatenate([aq, ak, av, bq, bkr], axis=1).T.astype(BF16)
    wn = jnp.concatenate([bkv, ag, bg, ma, mb], axis=1).astype(BF16)

    qd = B_NOPE + B_ROPE
    bqup = b_q_up.reshape(B_Q_RANK, B_HEADS, qd)
    bqup = jnp.pad(bqup, ((0, 0), (0, 0), (0, QK_PAD - qd))).reshape(B_Q_RANK, B_HEADS * QK_PAD)

    kvup = b_kv_up.reshape(B_KV_RANK, B_HEADS, B_NOPE + B_V)
    k_part = jnp.pad(kvup[:, :, :B_NOPE], ((0, 0), (0, 0), (0, QK_PAD - B_NOPE)))
    eye = jnp.eye(LANES, QK_PAD, k=B_NOPE, dtype=F32)[:, None, :]
    eye = jnp.where(jnp.arange(LANES)[:, None, None] <= B_ROPE, eye, 0.0)
    r_part = jnp.broadcast_to(eye, (LANES, B_HEADS, QK_PAD))
    waug = jnp.concatenate([k_part, r_part], axis=0).reshape(B_KV_RANK + LANES, B_HEADS * QK_PAD)
    bvup = kvup[:, :, B_NOPE:].reshape(B_KV_RANK, B_HEADS * B_V).T

    return {
        "pre_g": pre_g.reshape(1, D_MODEL),
        "wt": wt, "wn": wn,
        "aqg": a_q_g.reshape(A_HEAD_DIM, 1), "akg": a_k_g.reshape(A_HEAD_DIM, 1),
        "bqg": b_q_g.reshape(B_Q_RANK, 1), "bkvg": b_kv_g.reshape(1, B_KV_RANK),
        "bqup": bqup.T.astype(BF16), "waug": waug.astype(BF16), "bvup": bvup.astype(BF16),
    }


def _layer(x, mod3, pw, post_g, a_out, b_out, w_o):
    n = x.shape[1]
    tables = _rope_tables_t(n, A_HEAD_DIM) + _rope_tables_t(n, B_ROPE)
    qa, ka, va, ag, bg, ma, mb, qb, kb, vb, kna, knb, qna, qnb = _proj_call(x, mod3, pw, tables)
    ya = _attn_call(qa, ka, va, qna, kna, heads=A_HEADS, q_rows=A_HEAD_DIM, q_feat=A_HEAD_DIM,
                    group=A_HEADS // A_KV_HEADS, k_4d=True, three_stage=False, name="attn_gqa")
    yb = _attn_call(qb, kb, vb, qnb, knb, heads=B_HEADS, q_rows=QK_PAD, q_feat=B_NOPE + B_ROPE,
                    group=1, k_4d=False, three_stage=True, name="attn_mla")
    return _out_call(x, mod3, post_g, ya, yb, ag, bg, ma, mb, a_out, b_out, w_o)


def kernel(x_prompt, x_sample, c_prompt, c_sample, ada_w, ada_b, pre_norm_g, post_norm_g, w_in,
           a_q_norm_g, a_k_norm_g, b_q_norm_g, b_q_up, b_kv_norm_g, b_kv_up, a_out, b_out, w_o):
    assert ada_w.shape[0] == 1, "single layer"
    bp, bs = c_prompt.shape[0], c_sample.shape[0]
    rows = -(-(bp + bs) // 8) * 8
    c_all = jnp.concatenate([c_prompt, c_sample, jnp.zeros((rows - bp - bs, D_MODEL), F32)], axis=0)
    mod = _mod_call(c_all, ada_w[0].astype(BF16), ada_b[0].reshape(1, 3 * D_MODEL))
    mod3 = mod.reshape(rows, 3, D_MODEL)

    pw = _prep_weights(pre_norm_g[0], w_in[0], a_q_norm_g[0], a_k_norm_g[0], b_q_norm_g[0],
                       b_q_up[0], b_kv_norm_g[0], b_kv_up[0])
    post_g = post_norm_g[0].reshape(1, D_MODEL)
    a_o, b_o, w_oo = a_out[0].astype(BF16), b_out[0].astype(BF16), w_o[0].astype(BF16)

    y_prompt = _layer(x_prompt, mod3[:bp], pw, post_g, a_o, b_o, w_oo)
    y_sample = _layer(x_sample, mod3[bp:bp + bs], pw, post_g, a_o, b_o, w_oo)
    return (y_prompt, y_sample)
```

```python
import functools

import jax
import jax.numpy as jnp
from jax import lax
from jax.experimental import pallas as pl
from jax.experimental.pallas import tpu as pltpu

D_MODEL = 1024
GRID_W = 64
ROPE_THETA = 10000.0
EPS = 1e-6

A_HEADS = 8
A_KV_HEADS = 2
A_HEAD_DIM = 64
A_WIDTH = A_HEADS * A_HEAD_DIM

B_HEADS = 8
B_NOPE = 64
B_ROPE = 32
B_V = 64
B_Q_RANK = 384
B_KV_RANK = 256
B_WIDTH = B_HEADS * B_V

IN_SIZES = (A_WIDTH, A_KV_HEADS * A_HEAD_DIM, A_KV_HEADS * A_HEAD_DIM, A_WIDTH,
            B_Q_RANK, B_KV_RANK, B_ROPE, B_WIDTH, D_MODEL, D_MODEL)

LOG2E = 1.4426950408889634
LANES = 128
QK_PAD = 128
V_ROWS = 64
SHIFT_ROWS = 16
PROJ_TILE = 256
ATTN_TQ = 512
ATTN_TK = 256
ATTN_DEPTH = 2
ATTN_LAG = 2
ATTN_UNROLL = 30
SAFE_SCORE_BOUND = 60.0
BOUND_SLACK = 1.0625
VMEM_LIMIT = 56 * 1024 * 1024

F32 = jnp.float32
BF16 = jnp.bfloat16


def _rope_tables_t(n, d_rot):
    rows = n // GRID_W
    row_ids = jnp.repeat(jnp.arange(rows, dtype=F32), GRID_W)
    col_ids = jnp.tile(jnp.arange(GRID_W, dtype=F32), rows)
    d_axis = d_rot // 2
    inv = ROPE_THETA ** (-jnp.arange(0, d_axis, 2, dtype=F32) / d_axis)
    ang = jnp.concatenate([row_ids[:, None] * inv, col_ids[:, None] * inv], axis=-1)
    return jnp.cos(ang).T, jnp.sin(ang).T


def _const_spec(shape):
    nd = len(shape)
    return pl.BlockSpec(shape, lambda *_: (0,) * nd, pipeline_mode=pl.Buffered(1))


def _mod_kernel(c_ref, w_ref, b_ref, o_ref):
    c = c_ref[...]
    sc = (c * jax.nn.sigmoid(c)).astype(BF16)
    o_ref[...] = jnp.dot(sc, w_ref[...], preferred_element_type=F32) + b_ref[...]


def _mod_call(c_pad, ada_w, ada_b):
    rows = c_pad.shape[0]
    return pl.pallas_call(
        _mod_kernel,
        out_shape=jax.ShapeDtypeStruct((rows, 3 * D_MODEL), F32),
        name="adaln_mod",
    )(c_pad, ada_w, ada_b)


def _rope_t(x, cos, sin):
    half = x.shape[0] // 2
    x1, x2 = x[:half], x[half:]
    return jnp.concatenate([x1 * cos - x2 * sin, x2 * cos + x1 * sin], axis=0)


def _rms_t(x, g_col):
    ms = jnp.mean(x * x, axis=0, keepdims=True)
    return x * lax.rsqrt(ms + EPS) * g_col


def _proj_kernel(x_ref, mod_ref, pre_g_ref, wt_ref, wn_ref, aqg_ref, akg_ref, bqg_ref, bkvg_ref,
                 bqup_ref, waug_ref, bvup_ref, cos_a_ref, sin_a_ref, cos_b_ref, sin_b_ref,
                 qa_ref, ka_ref, va_ref, ag_ref, bg_ref, ma_ref, mb_ref, qb_ref, kb_ref, vb_ref,
                 kna_ref, knb_ref, qna_ref, qnb_ref):
    t = x_ref.shape[1]

    def tile_max_sq(x_t):
        sq = jnp.sum(x_t * x_t, axis=0, keepdims=True)
        return jnp.broadcast_to(jnp.max(sq, axis=1, keepdims=True), (8, LANES))

    one_row = (lax.broadcasted_iota(jnp.int32, (8, t), 0) == 0).astype(F32)
    x = x_ref[0]
    shift = mod_ref[0, 0:1, :]
    scale = mod_ref[0, 1:2, :]
    ms = jnp.mean(x * x, axis=-1, keepdims=True)
    h = (x * lax.rsqrt(ms + EPS) * pre_g_ref[...]) * (1.0 + scale) + shift
    hb = h.astype(BF16)

    t1 = lax.dot_general(wt_ref[...], hb, (((1,), (1,)), ((), ())), preferred_element_type=F32)
    cos_a, sin_a = cos_a_ref[...], sin_a_ref[...]
    cos_b, sin_b = cos_b_ref[...], sin_b_ref[...]
    aqg, akg = aqg_ref[...], akg_ref[...]

    q_scale = (A_HEAD_DIM ** -0.5) * LOG2E
    for hd in range(A_HEADS):
        q = t1[hd * A_HEAD_DIM:(hd + 1) * A_HEAD_DIM]
        q = _rope_t(_rms_t(q, aqg), cos_a, sin_a) * q_scale
        qa_ref[0, hd * A_HEAD_DIM:(hd + 1) * A_HEAD_DIM, :] = q.astype(BF16)
        qna_ref[0, hd, 0] = tile_max_sq(q)
    off = A_WIDTH
    for g in range(A_KV_HEADS):
        k = t1[off + g * A_HEAD_DIM:off + (g + 1) * A_HEAD_DIM]
        k = _rope_t(_rms_t(k, akg), cos_a, sin_a)
        slab = jnp.concatenate([k, one_row, jnp.zeros((QK_PAD - A_HEAD_DIM - 8, t), F32)], axis=0)
        ka_ref[0, g] = slab.T.astype(BF16)
        kna_ref[0, g, 0] = tile_max_sq(k)
    off += A_KV_HEADS * A_HEAD_DIM
    for g in range(A_KV_HEADS):
        va_ref[0, g, 0] = t1[off + g * A_HEAD_DIM:off + (g + 1) * A_HEAD_DIM].astype(BF16)
    off += A_KV_HEADS * A_HEAD_DIM

    bqn = _rms_t(t1[off:off + B_Q_RANK], bqg_ref[...]).astype(BF16)
    off += B_Q_RANK
    qb = jnp.dot(bqup_ref[...], bqn, preferred_element_type=F32)
    qb_scale = ((B_NOPE + B_ROPE) ** -0.5) * LOG2E
    for hd in range(B_HEADS):
        blk = qb[hd * QK_PAD:(hd + 1) * QK_PAD]
        roped = _rope_t(blk[B_NOPE:B_NOPE + B_ROPE], cos_b, sin_b)
        full = jnp.concatenate([blk[:B_NOPE], roped, blk[B_NOPE + B_ROPE:]], axis=0) * qb_scale
        qb_ref[0, hd * QK_PAD:(hd + 1) * QK_PAD, :] = full.astype(BF16)
        qnb_ref[0, hd, 0] = tile_max_sq(full)

    kr = _rope_t(t1[off:off + B_ROPE], cos_b, sin_b)
    kr_slab = jnp.concatenate([kr, one_row, jnp.zeros((LANES - B_ROPE - 8, t), F32)], axis=0).T

    bkv = jnp.dot(hb, wn_ref[:, 0:B_KV_RANK], preferred_element_type=F32)
    ms = jnp.mean(bkv * bkv, axis=-1, keepdims=True)
    bkvn = bkv * lax.rsqrt(ms + EPS) * bkvg_ref[...]
    lhs = jnp.concatenate([bkvn, kr_slab], axis=1).astype(BF16)
    kb = jnp.dot(lhs, waug_ref[...], preferred_element_type=F32)
    kb_ref[0] = kb.astype(BF16)
    for hd in range(B_HEADS):
        blk = kb[:, hd * QK_PAD:(hd + 1) * QK_PAD]
        ksq = jnp.sum(blk * blk, axis=1, keepdims=True) - 1.0
        knb_ref[0, hd, 0] = jnp.broadcast_to(jnp.max(ksq, axis=0, keepdims=True), (8, LANES))
    vb = lax.dot_general(bvup_ref[...], bkvn.astype(BF16), (((1,), (1,)), ((), ())),
                         preferred_element_type=F32)
    for hd in range(B_HEADS):
        vb_ref[0, hd, 0] = vb[hd * B_V:(hd + 1) * B_V].astype(BF16)

    c0 = B_KV_RANK
    ag = jnp.dot(hb, wn_ref[:, c0:c0 + A_WIDTH], preferred_element_type=F32)
    ag_ref[0] = (ag * jax.nn.sigmoid(ag)).astype(BF16)
    c0 += A_WIDTH
    bg = jnp.dot(hb, wn_ref[:, c0:c0 + B_WIDTH], preferred_element_type=F32)
    bg_ref[0] = (bg * jax.nn.sigmoid(bg)).astype(BF16)
    c0 += B_WIDTH
    ma = jnp.dot(hb, wn_ref[:, c0:c0 + D_MODEL], preferred_element_type=F32)
    ma_ref[0] = jax.nn.sigmoid(ma).astype(BF16)
    c0 += D_MODEL
    mb = jnp.dot(hb, wn_ref[:, c0:c0 + D_MODEL], preferred_element_type=F32)
    mb_ref[0] = jax.nn.sigmoid(mb).astype(BF16)


def _proj_call(x, mod3, pw, tables):
    bsz, n, _ = x.shape
    t = PROJ_TILE
    nt = n // t
    cos_a, sin_a, cos_b, sin_b = tables
    tok = lambda b, i: (b, i, 0)
    feat = lambda b, i: (b, 0, i)
    in_specs = [
        pl.BlockSpec((1, t, D_MODEL), tok),
        pl.BlockSpec((1, 3, D_MODEL), lambda b, i: (b, 0, 0)),
        _const_spec((1, D_MODEL)),
        _const_spec(pw["wt"].shape),
        _const_spec(pw["wn"].shape),
        _const_spec((A_HEAD_DIM, 1)),
        _const_spec((A_HEAD_DIM, 1)),
        _const_spec((B_Q_RANK, 1)),
        _const_spec((1, B_KV_RANK)),
        _const_spec(pw["bqup"].shape),
        _const_spec(pw["waug"].shape),
        _const_spec(pw["bvup"].shape),
        pl.BlockSpec((A_HEAD_DIM // 2, t), lambda b, i: (0, i)),
        pl.BlockSpec((A_HEAD_DIM // 2, t), lambda b, i: (0, i)),
        pl.BlockSpec((B_ROPE // 2, t), lambda b, i: (0, i)),
        pl.BlockSpec((B_ROPE // 2, t), lambda b, i: (0, i)),
    ]
    out_shape = [
        jax.ShapeDtypeStruct((bsz, A_WIDTH, n), BF16),
        jax.ShapeDtypeStruct((bsz, A_KV_HEADS, n, QK_PAD), BF16),
        jax.ShapeDtypeStruct((bsz, A_KV_HEADS, nt, V_ROWS, t), BF16),
        jax.ShapeDtypeStruct((bsz, n, A_WIDTH), BF16),
        jax.ShapeDtypeStruct((bsz, n, B_WIDTH), BF16),
        jax.ShapeDtypeStruct((bsz, n, D_MODEL), BF16),
        jax.ShapeDtypeStruct((bsz, n, D_MODEL), BF16),
        jax.ShapeDtypeStruct((bsz, B_HEADS * QK_PAD, n), BF16),
        jax.ShapeDtypeStruct((bsz, n, B_HEADS * QK_PAD), BF16),
        jax.ShapeDtypeStruct((bsz, B_HEADS, nt, V_ROWS, t), BF16),
        jax.ShapeDtypeStruct((bsz, A_KV_HEADS, nt, 8, LANES), F32),
        jax.ShapeDtypeStruct((bsz, B_HEADS, nt, 8, LANES), F32),
        jax.ShapeDtypeStruct((bsz, A_HEADS, nt, 8, LANES), F32),
        jax.ShapeDtypeStruct((bsz, B_HEADS, nt, 8, LANES), F32),
    ]
    out_specs = [
        pl.BlockSpec((1, A_WIDTH, t), feat),
        pl.BlockSpec((1, A_KV_HEADS, t, QK_PAD), lambda b, i: (b, 0, i, 0)),
        pl.BlockSpec((1, A_KV_HEADS, 1, V_ROWS, t), lambda b, i: (b, 0, i, 0, 0)),
        pl.BlockSpec((1, t, A_WIDTH), tok),
        pl.BlockSpec((1, t, B_WIDTH), tok),
        pl.BlockSpec((1, t, D_MODEL), tok),
        pl.BlockSpec((1, t, D_MODEL), tok),
        pl.BlockSpec((1, B_HEADS * QK_PAD, t), feat),
        pl.BlockSpec((1, t, B_HEADS * QK_PAD), tok),
        pl.BlockSpec((1, B_HEADS, 1, V_ROWS, t), lambda b, i: (b, 0, i, 0, 0)),
        pl.BlockSpec((1, A_KV_HEADS, 1, 8, LANES), lambda b, i: (b, 0, i, 0, 0)),
        pl.BlockSpec((1, B_HEADS, 1, 8, LANES), lambda b, i: (b, 0, i, 0, 0)),
        pl.BlockSpec((1, A_HEADS, 1, 8, LANES), lambda b, i: (b, 0, i, 0, 0)),
        pl.BlockSpec((1, B_HEADS, 1, 8, LANES), lambda b, i: (b, 0, i, 0, 0)),
    ]
    return pl.pallas_call(
        _proj_kernel,
        grid=(bsz, nt),
        in_specs=in_specs,
        out_specs=out_specs,
        out_shape=out_shape,
        compiler_params=pltpu.CompilerParams(
            dimension_semantics=("arbitrary", "arbitrary"), vmem_limit_bytes=VMEM_LIMIT),
        name="in_proj",
    )(x, mod3, pw["pre_g"], pw["wt"], pw["wn"], pw["aqg"], pw["akg"], pw["bqg"], pw["bkvg"],
      pw["bqup"], pw["waug"], pw["bvup"], cos_a, sin_a, cos_b, sin_b)


def _attn_kernel(small_ref, q_ref, k_ref, v_ref, kn_ref, o_ref, *s_refs, n_steps, k_4d, q_feat, heads):
    depth = ATTN_DEPTH
    q_in = q_ref[0]
    tq = q_in.shape[1]
    q = q_in[:q_feat]
    sub = ATTN_TK // PROJ_TILE
    tail_rows = QK_PAD - q_feat - SHIFT_ROWS
    scores_are_small = small_ref[pl.program_id(0) * heads + pl.program_id(1)] == 1

    qf = q.astype(F32)
    kmax2 = kn_ref[0, 0, 0:1, 0:1]
    bound = jnp.sqrt(jnp.sum(qf * qf, axis=0, keepdims=True) * kmax2) * BOUND_SLACK

    def with_shift_row(row):
        first = lax.broadcasted_iota(jnp.int32, (SHIFT_ROWS, tq), 0) == 0
        rows = jnp.where(first, row, 0.0).astype(BF16)
        parts = [q, rows]
        if tail_rows:
            parts.append(jnp.zeros((tail_rows, tq), BF16))
        return jnp.concatenate(parts, axis=0)

    def key_chunk(j):
        off = pl.multiple_of(j * ATTN_TK, ATTN_TK)
        if k_4d:
            return k_ref[0, 0, pl.ds(off, ATTN_TK), :]
        return k_ref[0, pl.ds(off, ATTN_TK), :]

    def value_chunk(j):
        return jnp.concatenate([v_ref[0, 0, j * sub + c] for c in range(sub)], axis=1)

    def column_sums(p):
        return jnp.sum(p.reshape(ATTN_TK // 8, 8, tq), axis=0)

    def finish(acc, lsum):
        o_ref[0] = acc / jnp.sum(lsum, axis=0, keepdims=True)

    acc0 = jnp.zeros((V_ROWS, tq), F32)
    lsum0 = jnp.zeros((8, tq), F32)

    @pl.when(scores_are_small)
    def _():
        q_aug = with_shift_row(-bound)

        tiles = ATTN_LAG + 1

        def scores(j, slot):
            s_refs[slot % tiles][...] = jnp.dot(key_chunk(j), q_aug, preferred_element_type=F32)

        def stages(j0, slot0, carry, count, n_scored):
            acc, lsum = carry
            for u in range(count):
                if u < n_scored:
                    scores(j0 + u + ATTN_LAG, slot0 + u + ATTN_LAG)
                p = jnp.exp2(s_refs[(slot0 + u) % tiles][...])
                acc = acc + jnp.dot(value_chunk(j0 + u), p.astype(BF16), preferred_element_type=F32)
                lsum = lsum + column_sums(p)
            return acc, lsum

        for j in range(ATTN_LAG):
            scores(j, j)
        trips = (n_steps - ATTN_LAG) // ATTN_UNROLL
        carry = lax.fori_loop(
            0, trips, lambda jj, c: stages(jj * ATTN_UNROLL, 0, c, ATTN_UNROLL, ATTN_UNROLL),
            (acc0, lsum0))
        done = trips * ATTN_UNROLL
        rest = n_steps - done
        finish(*stages(done, done, carry, rest, rest - ATTN_LAG))

    @pl.when(jnp.logical_not(scores_are_small))
    def _():
        q_pad = with_shift_row(jnp.zeros((1, tq), F32))

        def scores(j, s_ref):
            s = jnp.dot(key_chunk(j), q_pad, preferred_element_type=F32)
            s_ref[...] = s
            return jnp.max(s, axis=0, keepdims=True)

        def accumulate(j, s_ref, m, cmax, acc, lsum):
            m_new = jnp.maximum(m, cmax)
            alpha = jnp.exp2(m - m_new)
            p = jnp.exp2(s_ref[...] - m_new)
            acc = alpha * acc + jnp.dot(value_chunk(j), p.astype(BF16), preferred_element_type=F32)
            return m_new, acc, alpha * lsum + column_sums(p)

        def body(jj, carry):
            m, cmax, acc, lsum = carry
            for u in range(depth):
                j = jj * depth + u
                cmax_next = scores(jnp.minimum(j + 1, n_steps - 1), s_refs[(u + 1) % depth])
                m, acc, lsum = accumulate(j, s_refs[u], m, cmax, acc, lsum)
                cmax = cmax_next
            return m, cmax, acc, lsum

        m0 = jnp.full((1, tq), -1e30, F32)
        cmax0 = scores(0, s_refs[0])
        _, _, acc, lsum = lax.fori_loop(0, n_steps // depth, body, (m0, cmax0, acc0, lsum0))
        finish(acc, lsum)


def _attn_call(q, k, v, qn_tiles, kn_tiles, *, heads, q_rows, q_feat, group, k_4d, name):
    bsz, _, n = q.shape
    tq = ATTN_TQ
    n_chunks = v.shape[2]
    kn = jnp.max(kn_tiles, axis=2)
    qmax2 = jnp.max(qn_tiles, axis=(2, 3, 4))
    kmax2 = jnp.repeat(jnp.max(kn, axis=(2, 3)), group, axis=1)
    small = (qmax2 * kmax2 * BOUND_SLACK ** 2 <= SAFE_SCORE_BOUND ** 2).astype(jnp.int32).reshape(-1)
    if k_4d:
        k_spec = pl.BlockSpec((1, 1, n, QK_PAD), lambda b, h, i, _: (b, h // group, 0, 0))
    else:
        k_spec = pl.BlockSpec((1, n, QK_PAD), lambda b, h, i, _: (b, 0, h))
    n_steps = n // ATTN_TK
    assert n_steps >= ATTN_LAG and n_steps % ATTN_DEPTH == 0 and ATTN_UNROLL % (ATTN_LAG + 1) == 0
    kernel = functools.partial(_attn_kernel, n_steps=n_steps, k_4d=k_4d, q_feat=q_feat, heads=heads)
    grid_spec = pltpu.PrefetchScalarGridSpec(
        num_scalar_prefetch=1,
        grid=(bsz, heads, n // tq),
        in_specs=[
            pl.BlockSpec((1, q_rows, tq), lambda b, h, i, _: (b, h, i)),
            k_spec,
            pl.BlockSpec((1, 1, n_chunks, V_ROWS, PROJ_TILE), lambda b, h, i, _: (b, h // group, 0, 0, 0)),
            pl.BlockSpec((1, 1, 8, LANES), lambda b, h, i, _: (b, h // group, 0, 0)),
        ],
        out_specs=pl.BlockSpec((1, V_ROWS, tq), lambda b, h, i, _: (b, h, i)),
        scratch_shapes=[pltpu.VMEM((ATTN_TK, tq), F32)] * max(ATTN_LAG + 1, ATTN_DEPTH),
    )
    return pl.pallas_call(
        kernel,
        grid_spec=grid_spec,
        out_shape=jax.ShapeDtypeStruct((bsz, heads * V_ROWS, n), F32),
        compiler_params=pltpu.CompilerParams(
            dimension_semantics=("arbitrary", "arbitrary", "arbitrary"), vmem_limit_bytes=VMEM_LIMIT),
        name=name,
    )(small, q, k, v, kn)


def _out_kernel(x_ref, mod_ref, post_g_ref, ya_ref, yb_ref, ag_ref, bg_ref, ma_ref, mb_ref,
                aout_ref, bout_ref, wo_ref, y_ref):
    ga = (ya_ref[0].T * ag_ref[0].astype(F32)).astype(BF16)
    gb = (yb_ref[0].T * bg_ref[0].astype(F32)).astype(BF16)
    pa = jnp.dot(ga, aout_ref[...], preferred_element_type=F32)
    pb = jnp.dot(gb, bout_ref[...], preferred_element_type=F32)
    merged = ma_ref[0].astype(F32) * pa + mb_ref[0].astype(F32) * pb
    z = jnp.dot(merged.astype(BF16), wo_ref[...], preferred_element_type=F32)
    ms = jnp.mean(z * z, axis=-1, keepdims=True)
    zn = z * lax.rsqrt(ms + EPS) * post_g_ref[...]
    y_ref[0] = x_ref[0] + mod_ref[0, 2:3, :] * zn


def _out_call(x, mod3, post_g, ya, yb, ag, bg, ma, mb, a_out, b_out, w_o):
    bsz, n, _ = x.shape
    t = PROJ_TILE
    tok = lambda b, i: (b, i, 0)
    feat = lambda b, i: (b, 0, i)
    return pl.pallas_call(
        _out_kernel,
        grid=(bsz, n // t),
        in_specs=[
            pl.BlockSpec((1, t, D_MODEL), tok),
            pl.BlockSpec((1, 3, D_MODEL), lambda b, i: (b, 0, 0)),
            _const_spec((1, D_MODEL)),
            pl.BlockSpec((1, A_WIDTH, t), feat),
            pl.BlockSpec((1, B_WIDTH, t), feat),
            pl.BlockSpec((1, t, A_WIDTH), tok),
            pl.BlockSpec((1, t, B_WIDTH), tok),
            pl.BlockSpec((1, t, D_MODEL), tok),
            pl.BlockSpec((1, t, D_MODEL), tok),
            _const_spec(a_out.shape),
            _const_spec(b_out.shape),
            _const_spec(w_o.shape),
        ],
        out_specs=pl.BlockSpec((1, t, D_MODEL), tok),
        out_shape=jax.ShapeDtypeStruct((bsz, n, D_MODEL), F32),
        compiler_params=pltpu.CompilerParams(
            dimension_semantics=("arbitrary", "arbitrary"), vmem_limit_bytes=VMEM_LIMIT),
        name="out_proj",
    )(x, mod3, post_g, ya, yb, ag, bg, ma, mb, a_out, b_out, w_o)


def _prep_weights(pre_g, w_in, a_q_g, a_k_g, b_q_g, b_q_up, b_kv_g, b_kv_up):
    pts = [0]
    for s in IN_SIZES:
        pts.append(pts[-1] + s)
    seg = lambda i: w_in[:, pts[i]:pts[i + 1]]
    aq, ak, av, ag, bq, bkv, bkr, bg, ma, mb = (seg(i) for i in range(10))
    wt = jnp.concatenate([aq, ak, av, bq, bkr], axis=1).T.astype(BF16)
    wn = jnp.concatenate([bkv, ag, bg, ma, mb], axis=1).astype(BF16)

    qd = B_NOPE + B_ROPE
    bqup = b_q_up.reshape(B_Q_RANK, B_HEADS, qd)
    bqup = jnp.pad(bqup, ((0, 0), (0, 0), (0, QK_PAD - qd))).reshape(B_Q_RANK, B_HEADS * QK_PAD)

    kvup = b_kv_up.reshape(B_KV_RANK, B_HEADS, B_NOPE + B_V)
    k_part = jnp.pad(kvup[:, :, :B_NOPE], ((0, 0), (0, 0), (0, QK_PAD - B_NOPE)))
    eye = jnp.eye(LANES, QK_PAD, k=B_NOPE, dtype=F32)[:, None, :]
    eye = jnp.where(jnp.arange(LANES)[:, None, None] <= B_ROPE, eye, 0.0)
    r_part = jnp.broadcast_to(eye, (LANES, B_HEADS, QK_PAD))
    waug = jnp.concatenate([k_part, r_part], axis=0).reshape(B_KV_RANK + LANES, B_HEADS * QK_PAD)
    bvup = kvup[:, :, B_NOPE:].reshape(B_KV_RANK, B_HEADS * B_V).T

    return {
        "pre_g": pre_g.reshape(1, D_MODEL),
        "wt": wt, "wn": wn,
        "aqg": a_q_g.reshape(A_HEAD_DIM, 1), "akg": a_k_g.reshape(A_HEAD_DIM, 1),
        "bqg": b_q_g.reshape(B_Q_RANK, 1), "bkvg": b_kv_g.reshape(1, B_KV_RANK),
        "bqup": bqup.T.astype(BF16), "waug": waug.astype(BF16), "bvup": bvup.astype(BF16),
    }


def _layer(x, mod3, pw, post_g, a_out, b_out, w_o):
    n = x.shape[1]
    tables = _rope_tables_t(n, A_HEAD_DIM) + _rope_tables_t(n, B_ROPE)
    qa, ka, va, ag, bg, ma, mb, qb, kb, vb, kna, knb, qna, qnb = _proj_call(x, mod3, pw, tables)
    ya = _attn_call(qa, ka, va, qna, kna, heads=A_HEADS, q_rows=A_HEAD_DIM, q_feat=A_HEAD_DIM,
                    group=A_HEADS // A_KV_HEADS, k_4d=True, name="attn_gqa")
    yb = _attn_call(qb, kb, vb, qnb, knb, heads=B_HEADS, q_rows=QK_PAD, q_feat=B_NOPE + B_ROPE,
                    group=1, k_4d=False, name="attn_mla")
    return _out_call(x, mod3, post_g, ya, yb, ag, bg, ma, mb, a_out, b_out, w_o)


def kernel(x_prompt, x_sample, c_prompt, c_sample, ada_w, ada_b, pre_norm_g, post_norm_g, w_in,
           a_q_norm_g, a_k_norm_g, b_q_norm_g, b_q_up, b_kv_norm_g, b_kv_up, a_out, b_out, w_o):
    assert ada_w.shape[0] == 1, "single layer"
    bp, bs = c_prompt.shape[0], c_sample.shape[0]
    rows = -(-(bp + bs) // 8) * 8
    c_all = jnp.concatenate([c_prompt, c_sample, jnp.zeros((rows - bp - bs, D_MODEL), F32)], axis=0)
    mod = _mod_call(c_all, ada_w[0].astype(BF16), ada_b[0].reshape(1, 3 * D_MODEL))
    mod3 = mod.reshape(rows, 3, D_MODEL)

    pw = _prep_weights(pre_norm_g[0], w_in[0], a_q_norm_g[0], a_k_norm_g[0], b_q_norm_g[0],
                       b_q_up[0], b_kv_norm_g[0], b_kv_up[0])
    post_g = post_norm_g[0].reshape(1, D_MODEL)
    a_o, b_o, w_oo = a_out[0].astype(BF16), b_out[0].astype(BF16), w_o[0].astype(BF16)

    y_prompt = _layer(x_prompt, mod3[:bp], pw, post_g, a_o, b_o, w_oo)
    y_sample = _layer(x_sample, mod3[bp:bp + bs], pw, post_g, a_o, b_o, w_oo)
    return (y_prompt, y_sample)
```

```python
import functools

import jax
import jax.numpy as jnp
from jax import lax
from jax.experimental import pallas as pl
from jax.experimental.pallas import tpu as pltpu

D_MODEL = 1024
GRID_W = 64
ROPE_THETA = 10000.0
EPS = 1e-6

A_HEADS = 8
A_KV_HEADS = 2
A_HEAD_DIM = 64
A_WIDTH = A_HEADS * A_HEAD_DIM

B_HEADS = 8
B_NOPE = 64
B_ROPE = 32
B_V = 64
B_Q_RANK = 384
B_KV_RANK = 256
B_WIDTH = B_HEADS * B_V

IN_SIZES = (A_WIDTH, A_KV_HEADS * A_HEAD_DIM, A_KV_HEADS * A_HEAD_DIM, A_WIDTH,
            B_Q_RANK, B_KV_RANK, B_ROPE, B_WIDTH, D_MODEL, D_MODEL)

LOG2E = 1.4426950408889634
LANES = 128
QK_PAD = 128
V_ROWS = 64
SHIFT_ROWS = 16
PROJ_TILE = 512
V_CHUNK = 256
OUT_TILE = 512
ATTN_TQ = 512
ATTN_TK = 256
ATTN_DEPTH = 2
ATTN_LAG = 2
ATTN_UNROLL = 60
SAFE_SCORE_BOUND = 60.0
BOUND_SLACK = 1.0625
VMEM_LIMIT = 56 * 1024 * 1024

F32 = jnp.float32
BF16 = jnp.bfloat16


def _rope_tables_t(n, d_rot):
    rows = n // GRID_W
    row_ids = jnp.repeat(jnp.arange(rows, dtype=F32), GRID_W)
    col_ids = jnp.tile(jnp.arange(GRID_W, dtype=F32), rows)
    d_axis = d_rot // 2
    inv = ROPE_THETA ** (-jnp.arange(0, d_axis, 2, dtype=F32) / d_axis)
    ang = jnp.concatenate([row_ids[:, None] * inv, col_ids[:, None] * inv], axis=-1)
    return jnp.cos(ang).T, jnp.sin(ang).T


def _const_spec(shape):
    nd = len(shape)
    return pl.BlockSpec(shape, lambda *_: (0,) * nd, pipeline_mode=pl.Buffered(1))


def _mod_kernel(c_ref, w_ref, b_ref, o_ref):
    c = c_ref[...]
    sc = (c * jax.nn.sigmoid(c)).astype(BF16)
    o_ref[...] = jnp.dot(sc, w_ref[...], preferred_element_type=F32) + b_ref[...]


def _mod_call(c_pad, ada_w, ada_b):
    rows = c_pad.shape[0]
    return pl.pallas_call(
        _mod_kernel,
        out_shape=jax.ShapeDtypeStruct((rows, 3 * D_MODEL), F32),
        name="adaln_mod",
    )(c_pad, ada_w, ada_b)


def _rope_t(x, cos, sin):
    half = x.shape[0] // 2
    x1, x2 = x[:half], x[half:]
    return jnp.concatenate([x1 * cos - x2 * sin, x2 * cos + x1 * sin], axis=0)


def _rms_t(x, g_col):
    ms = jnp.mean(x * x, axis=0, keepdims=True)
    return x * lax.rsqrt(ms + EPS) * g_col


def _proj_kernel(x_ref, mod_ref, pre_g_ref, wt_ref, wn_ref, aqg_ref, akg_ref, bqg_ref, bkvg_ref,
                 bqup_ref, waug_ref, bvup_ref, cos_a_ref, sin_a_ref, cos_b_ref, sin_b_ref,
                 qa_ref, ka_ref, va_ref, ag_ref, bg_ref, ma_ref, mb_ref, qb_ref, kb_ref, vb_ref,
                 kna_ref, knb_ref, qna_ref, qnb_ref):
    t = x_ref.shape[1]

    def tile_max_sq(x_t):
        sq = jnp.sum(x_t * x_t, axis=0, keepdims=True)
        return jnp.broadcast_to(jnp.max(sq, axis=1, keepdims=True), (8, LANES))

    one_row = (lax.broadcasted_iota(jnp.int32, (8, t), 0) == 0).astype(F32)
    x = x_ref[0]
    shift = mod_ref[0, 0:1, :]
    scale = mod_ref[0, 1:2, :]
    ms = jnp.mean(x * x, axis=-1, keepdims=True)
    h = (x * lax.rsqrt(ms + EPS) * pre_g_ref[...]) * (1.0 + scale) + shift
    hb = h.astype(BF16)

    t1 = lax.dot_general(wt_ref[...], hb, (((1,), (1,)), ((), ())), preferred_element_type=F32)
    cos_a, sin_a = cos_a_ref[...], sin_a_ref[...]
    cos_b, sin_b = cos_b_ref[...], sin_b_ref[...]
    aqg, akg = aqg_ref[...], akg_ref[...]

    q_scale = (A_HEAD_DIM ** -0.5) * LOG2E
    for hd in range(A_HEADS):
        q = t1[hd * A_HEAD_DIM:(hd + 1) * A_HEAD_DIM]
        q = _rope_t(_rms_t(q, aqg), cos_a, sin_a) * q_scale
        qa_ref[0, hd * A_HEAD_DIM:(hd + 1) * A_HEAD_DIM, :] = q.astype(BF16)
        qna_ref[0, hd, 0] = tile_max_sq(q)
    off = A_WIDTH
    for g in range(A_KV_HEADS):
        k = t1[off + g * A_HEAD_DIM:off + (g + 1) * A_HEAD_DIM]
        k = _rope_t(_rms_t(k, akg), cos_a, sin_a)
        slab = jnp.concatenate([k, one_row, jnp.zeros((QK_PAD - A_HEAD_DIM - 8, t), F32)], axis=0)
        ka_ref[0, g] = slab.T.astype(BF16)
        kna_ref[0, g, 0] = tile_max_sq(k)
    off += A_KV_HEADS * A_HEAD_DIM
    for g in range(A_KV_HEADS):
        v = t1[off + g * A_HEAD_DIM:off + (g + 1) * A_HEAD_DIM].astype(BF16)
        for c in range(t // V_CHUNK):
            va_ref[0, g, c] = v[:, c * V_CHUNK:(c + 1) * V_CHUNK]
    off += A_KV_HEADS * A_HEAD_DIM

    bqn = _rms_t(t1[off:off + B_Q_RANK], bqg_ref[...]).astype(BF16)
    off += B_Q_RANK
    qb = jnp.dot(bqup_ref[...], bqn, preferred_element_type=F32)
    qb_scale = ((B_NOPE + B_ROPE) ** -0.5) * LOG2E
    for hd in range(B_HEADS):
        blk = qb[hd * QK_PAD:(hd + 1) * QK_PAD]
        roped = _rope_t(blk[B_NOPE:B_NOPE + B_ROPE], cos_b, sin_b)
        full = jnp.concatenate([blk[:B_NOPE], roped, blk[B_NOPE + B_ROPE:]], axis=0) * qb_scale
        qb_ref[0, hd * QK_PAD:(hd + 1) * QK_PAD, :] = full.astype(BF16)
        qnb_ref[0, hd, 0] = tile_max_sq(full)

    kr = _rope_t(t1[off:off + B_ROPE], cos_b, sin_b)
    kr_slab = jnp.concatenate([kr, one_row, jnp.zeros((LANES - B_ROPE - 8, t), F32)], axis=0).T

    bkv = jnp.dot(hb, wn_ref[:, 0:B_KV_RANK], preferred_element_type=F32)
    ms = jnp.mean(bkv * bkv, axis=-1, keepdims=True)
    bkvn = bkv * lax.rsqrt(ms + EPS) * bkvg_ref[...]
    lhs = jnp.concatenate([bkvn, kr_slab], axis=1).astype(BF16)
    kb = jnp.dot(lhs, waug_ref[...], preferred_element_type=F32)
    for hd in range(B_HEADS):
        blk = kb[:, hd * QK_PAD:(hd + 1) * QK_PAD]
        kb_ref[0, hd] = blk.astype(BF16)
        ksq = jnp.sum(blk * blk, axis=1, keepdims=True) - 1.0
        knb_ref[0, hd, 0] = jnp.broadcast_to(jnp.max(ksq, axis=0, keepdims=True), (8, LANES))
    vb = lax.dot_general(bvup_ref[...], bkvn.astype(BF16), (((1,), (1,)), ((), ())),
                         preferred_element_type=F32)
    for hd in range(B_HEADS):
        v = vb[hd * B_V:(hd + 1) * B_V].astype(BF16)
        for c in range(t // V_CHUNK):
            vb_ref[0, hd, c] = v[:, c * V_CHUNK:(c + 1) * V_CHUNK]

    c0 = B_KV_RANK
    ag = jnp.dot(hb, wn_ref[:, c0:c0 + A_WIDTH], preferred_element_type=F32)
    ag_ref[0] = (ag * jax.nn.sigmoid(ag)).astype(BF16)
    c0 += A_WIDTH
    bg = jnp.dot(hb, wn_ref[:, c0:c0 + B_WIDTH], preferred_element_type=F32)
    bg_ref[0] = (bg * jax.nn.sigmoid(bg)).astype(BF16)
    c0 += B_WIDTH
    ma = jnp.dot(hb, wn_ref[:, c0:c0 + D_MODEL], preferred_element_type=F32)
    ma_ref[0] = jax.nn.sigmoid(ma).astype(BF16)
    c0 += D_MODEL
    mb = jnp.dot(hb, wn_ref[:, c0:c0 + D_MODEL], preferred_element_type=F32)
    mb_ref[0] = jax.nn.sigmoid(mb).astype(BF16)


def _proj_call(x, mod3, pw, tables):
    bsz, n, _ = x.shape
    t = PROJ_TILE
    nt = n // t
    cos_a, sin_a, cos_b, sin_b = tables
    tok = lambda b, i: (b, i, 0)
    feat = lambda b, i: (b, 0, i)
    in_specs = [
        pl.BlockSpec((1, t, D_MODEL), tok),
        pl.BlockSpec((1, 3, D_MODEL), lambda b, i: (b, 0, 0)),
        _const_spec((1, D_MODEL)),
        _const_spec(pw["wt"].shape),
        _const_spec(pw["wn"].shape),
        _const_spec((A_HEAD_DIM, 1)),
        _const_spec((A_HEAD_DIM, 1)),
        _const_spec((B_Q_RANK, 1)),
        _const_spec((1, B_KV_RANK)),
        _const_spec(pw["bqup"].shape),
        _const_spec(pw["waug"].shape),
        _const_spec(pw["bvup"].shape),
        pl.BlockSpec((A_HEAD_DIM // 2, t), lambda b, i: (0, i)),
        pl.BlockSpec((A_HEAD_DIM // 2, t), lambda b, i: (0, i)),
        pl.BlockSpec((B_ROPE // 2, t), lambda b, i: (0, i)),
        pl.BlockSpec((B_ROPE // 2, t), lambda b, i: (0, i)),
    ]
    out_shape = [
        jax.ShapeDtypeStruct((bsz, A_WIDTH, n), BF16),
        jax.ShapeDtypeStruct((bsz, A_KV_HEADS, n, QK_PAD), BF16),
        jax.ShapeDtypeStruct((bsz, A_KV_HEADS, n // V_CHUNK, V_ROWS, V_CHUNK), BF16),
        jax.ShapeDtypeStruct((bsz, n, A_WIDTH), BF16),
        jax.ShapeDtypeStruct((bsz, n, B_WIDTH), BF16),
        jax.ShapeDtypeStruct((bsz, n, D_MODEL), BF16),
        jax.ShapeDtypeStruct((bsz, n, D_MODEL), BF16),
        jax.ShapeDtypeStruct((bsz, B_HEADS * QK_PAD, n), BF16),
        jax.ShapeDtypeStruct((bsz, B_HEADS, n, QK_PAD), BF16),
        jax.ShapeDtypeStruct((bsz, B_HEADS, n // V_CHUNK, V_ROWS, V_CHUNK), BF16),
        jax.ShapeDtypeStruct((bsz, A_KV_HEADS, nt, 8, LANES), F32),
        jax.ShapeDtypeStruct((bsz, B_HEADS, nt, 8, LANES), F32),
        jax.ShapeDtypeStruct((bsz, A_HEADS, nt, 8, LANES), F32),
        jax.ShapeDtypeStruct((bsz, B_HEADS, nt, 8, LANES), F32),
    ]
    out_specs = [
        pl.BlockSpec((1, A_WIDTH, t), feat),
        pl.BlockSpec((1, A_KV_HEADS, t, QK_PAD), lambda b, i: (b, 0, i, 0)),
        pl.BlockSpec((1, A_KV_HEADS, t // V_CHUNK, V_ROWS, V_CHUNK), lambda b, i: (b, 0, i, 0, 0)),
        pl.BlockSpec((1, t, A_WIDTH), tok),
        pl.BlockSpec((1, t, B_WIDTH), tok),
        pl.BlockSpec((1, t, D_MODEL), tok),
        pl.BlockSpec((1, t, D_MODEL), tok),
        pl.BlockSpec((1, B_HEADS * QK_PAD, t), feat),
        pl.BlockSpec((1, B_HEADS, t, QK_PAD), lambda b, i: (b, 0, i, 0)),
        pl.BlockSpec((1, B_HEADS, t // V_CHUNK, V_ROWS, V_CHUNK), lambda b, i: (b, 0, i, 0, 0)),
        pl.BlockSpec((1, A_KV_HEADS, 1, 8, LANES), lambda b, i: (b, 0, i, 0, 0)),
        pl.BlockSpec((1, B_HEADS, 1, 8, LANES), lambda b, i: (b, 0, i, 0, 0)),
        pl.BlockSpec((1, A_HEADS, 1, 8, LANES), lambda b, i: (b, 0, i, 0, 0)),
        pl.BlockSpec((1, B_HEADS, 1, 8, LANES), lambda b, i: (b, 0, i, 0, 0)),
    ]
    return pl.pallas_call(
        _proj_kernel,
        grid=(bsz, nt),
        in_specs=in_specs,
        out_specs=out_specs,
        out_shape=out_shape,
        compiler_params=pltpu.CompilerParams(
            dimension_semantics=("arbitrary", "arbitrary"), vmem_limit_bytes=VMEM_LIMIT),
        name="in_proj",
    )(x, mod3, pw["pre_g"], pw["wt"], pw["wn"], pw["aqg"], pw["akg"], pw["bqg"], pw["bkvg"],
      pw["bqup"], pw["waug"], pw["bvup"], cos_a, sin_a, cos_b, sin_b)


def _attn_kernel(small_ref, q_ref, k_ref, v_ref, kn_ref, o_ref, *s_refs, n_steps, q_feat, heads):
    depth = ATTN_DEPTH
    q_in = q_ref[0]
    tq = q_in.shape[1]
    q = q_in[:q_feat]
    sub = ATTN_TK // V_CHUNK
    tail_rows = QK_PAD - q_feat - SHIFT_ROWS
    scores_are_small = small_ref[pl.program_id(0) * heads + pl.program_id(1)] == 1

    qf = q.astype(F32)
    kmax2 = kn_ref[0, 0, 0:1, 0:1]
    bound = jnp.sqrt(jnp.sum(qf * qf, axis=0, keepdims=True) * kmax2) * BOUND_SLACK

    def with_shift_row(row):
        first = lax.broadcasted_iota(jnp.int32, (SHIFT_ROWS, tq), 0) == 0
        rows = jnp.where(first, row, 0.0).astype(BF16)
        parts = [q, rows]
        if tail_rows:
            parts.append(jnp.zeros((tail_rows, tq), BF16))
        return jnp.concatenate(parts, axis=0)

    def key_chunk(j):
        off = pl.multiple_of(j * ATTN_TK, ATTN_TK)
        return k_ref[0, 0, pl.ds(off, ATTN_TK), :]

    def value_chunk(j):
        return jnp.concatenate([v_ref[0, 0, j * sub + c] for c in range(sub)], axis=1)

    def column_sums(p):
        return jnp.sum(p.reshape(ATTN_TK // 8, 8, tq), axis=0)

    def finish(acc, lsum):
        o_ref[0] = acc / jnp.sum(lsum, axis=0, keepdims=True)

    acc0 = jnp.zeros((V_ROWS, tq), F32)
    lsum0 = jnp.zeros((8, tq), F32)

    @pl.when(scores_are_small)
    def _():
        q_aug = with_shift_row(-bound)

        tiles = ATTN_LAG + 1

        def scores(j, slot):
            s_refs[slot % tiles][...] = jnp.dot(key_chunk(j), q_aug, preferred_element_type=F32)

        def stages(j0, slot0, carry, count, n_scored):
            acc, lsum = carry
            for u in range(count):
                if u < n_scored:
                    scores(j0 + u + ATTN_LAG, slot0 + u + ATTN_LAG)
                p = jnp.exp2(s_refs[(slot0 + u) % tiles][...])
                acc = acc + jnp.dot(value_chunk(j0 + u), p.astype(BF16), preferred_element_type=F32)
                lsum = lsum + column_sums(p)
            return acc, lsum

        for j in range(ATTN_LAG):
            scores(j, j)
        trips = (n_steps - ATTN_LAG) // ATTN_UNROLL
        carry = lax.fori_loop(
            0, trips, lambda jj, c: stages(jj * ATTN_UNROLL, 0, c, ATTN_UNROLL, ATTN_UNROLL),
            (acc0, lsum0))
        done = trips * ATTN_UNROLL
        rest = n_steps - done
        finish(*stages(done, done, carry, rest, rest - ATTN_LAG))

    @pl.when(jnp.logical_not(scores_are_small))
    def _():
        q_pad = with_shift_row(jnp.zeros((1, tq), F32))

        def scores(j, s_ref):
            s = jnp.dot(key_chunk(j), q_pad, preferred_element_type=F32)
            s_ref[...] = s
            return jnp.max(s, axis=0, keepdims=True)

        def accumulate(j, s_ref, m, cmax, acc, lsum):
            m_new = jnp.maximum(m, cmax)
            alpha = jnp.exp2(m - m_new)
            p = jnp.exp2(s_ref[...] - m_new)
            acc = alpha * acc + jnp.dot(value_chunk(j), p.astype(BF16), preferred_element_type=F32)
            return m_new, acc, alpha * lsum + column_sums(p)

        def body(jj, carry):
            m, cmax, acc, lsum = carry
            for u in range(depth):
                j = jj * depth + u
                cmax_next = scores(jnp.minimum(j + 1, n_steps - 1), s_refs[(u + 1) % depth])
                m, acc, lsum = accumulate(j, s_refs[u], m, cmax, acc, lsum)
                cmax = cmax_next
            return m, cmax, acc, lsum

        m0 = jnp.full((1, tq), -1e30, F32)
        cmax0 = scores(0, s_refs[0])
        _, _, acc, lsum = lax.fori_loop(0, n_steps // depth, body, (m0, cmax0, acc0, lsum0))
        finish(acc, lsum)


def _attn_call(q, k, v, qn_tiles, kn_tiles, *, heads, q_rows, q_feat, group, name):
    bsz, _, n = q.shape
    tq = ATTN_TQ
    n_chunks = v.shape[2]
    kn = jnp.max(kn_tiles, axis=2)
    qmax2 = jnp.max(qn_tiles, axis=(2, 3, 4))
    kmax2 = jnp.repeat(jnp.max(kn, axis=(2, 3)), group, axis=1)
    small = (qmax2 * kmax2 * BOUND_SLACK ** 2 <= SAFE_SCORE_BOUND ** 2).astype(jnp.int32).reshape(-1)
    k_spec = pl.BlockSpec((1, 1, n, QK_PAD), lambda b, h, i, _: (b, h // group, 0, 0))
    n_steps = n // ATTN_TK
    assert n_steps >= ATTN_LAG and n_steps % ATTN_DEPTH == 0 and ATTN_UNROLL % (ATTN_LAG + 1) == 0
    kernel = functools.partial(_attn_kernel, n_steps=n_steps, q_feat=q_feat, heads=heads)
    grid_spec = pltpu.PrefetchScalarGridSpec(
        num_scalar_prefetch=1,
        grid=(bsz, heads, n // tq),
        in_specs=[
            pl.BlockSpec((1, q_rows, tq), lambda b, h, i, _: (b, h, i)),
            k_spec,
            pl.BlockSpec((1, 1, n_chunks, V_ROWS, V_CHUNK), lambda b, h, i, _: (b, h // group, 0, 0, 0)),
            pl.BlockSpec((1, 1, 8, LANES), lambda b, h, i, _: (b, h // group, 0, 0)),
        ],
        out_specs=pl.BlockSpec((1, V_ROWS, tq), lambda b, h, i, _: (b, h, i)),
        scratch_shapes=[pltpu.VMEM((ATTN_TK, tq), F32)] * max(ATTN_LAG + 1, ATTN_DEPTH),
    )
    return pl.pallas_call(
        kernel,
        grid_spec=grid_spec,
        out_shape=jax.ShapeDtypeStruct((bsz, heads * V_ROWS, n), F32),
        compiler_params=pltpu.CompilerParams(
            dimension_semantics=("arbitrary", "arbitrary", "arbitrary"), vmem_limit_bytes=VMEM_LIMIT),
        name=name,
    )(small, q, k, v, kn)


def _out_kernel(x_ref, mod_ref, post_g_ref, ya_ref, yb_ref, ag_ref, bg_ref, ma_ref, mb_ref,
                aout_ref, bout_ref, wo_ref, y_ref):
    ga = (ya_ref[0].T * ag_ref[0].astype(F32)).astype(BF16)
    gb = (yb_ref[0].T * bg_ref[0].astype(F32)).astype(BF16)
    pa = jnp.dot(ga, aout_ref[...], preferred_element_type=F32)
    pb = jnp.dot(gb, bout_ref[...], preferred_element_type=F32)
    merged = ma_ref[0].astype(F32) * pa + mb_ref[0].astype(F32) * pb
    z = jnp.dot(merged.astype(BF16), wo_ref[...], preferred_element_type=F32)
    ms = jnp.mean(z * z, axis=-1, keepdims=True)
    zn = z * lax.rsqrt(ms + EPS) * post_g_ref[...]
    y_ref[0] = x_ref[0] + mod_ref[0, 2:3, :] * zn


def _out_call(x, mod3, post_g, ya, yb, ag, bg, ma, mb, a_out, b_out, w_o):
    bsz, n, _ = x.shape
    t = OUT_TILE
    tok = lambda b, i: (b, i, 0)
    feat = lambda b, i: (b, 0, i)
    return pl.pallas_call(
        _out_kernel,
        grid=(bsz, n // t),
        in_specs=[
            pl.BlockSpec((1, t, D_MODEL), tok),
            pl.BlockSpec((1, 3, D_MODEL), lambda b, i: (b, 0, 0)),
            _const_spec((1, D_MODEL)),
            pl.BlockSpec((1, A_WIDTH, t), feat),
            pl.BlockSpec((1, B_WIDTH, t), feat),
            pl.BlockSpec((1, t, A_WIDTH), tok),
            pl.BlockSpec((1, t, B_WIDTH), tok),
            pl.BlockSpec((1, t, D_MODEL), tok),
            pl.BlockSpec((1, t, D_MODEL), tok),
            _const_spec(a_out.shape),
            _const_spec(b_out.shape),
            _const_spec(w_o.shape),
        ],
        out_specs=pl.BlockSpec((1, t, D_MODEL), tok),
        out_shape=jax.ShapeDtypeStruct((bsz, n, D_MODEL), F32),
        compiler_params=pltpu.CompilerParams(
            dimension_semantics=("arbitrary", "arbitrary"), vmem_limit_bytes=VMEM_LIMIT),
        name="out_proj",
    )(x, mod3, post_g, ya, yb, ag, bg, ma, mb, a_out, b_out, w_o)


def _prep_weights(pre_g, w_in, a_q_g, a_k_g, b_q_g, b_q_up, b_kv_g, b_kv_up):
    pts = [0]
    for s in IN_SIZES:
        pts.append(pts[-1] + s)
    seg = lambda i: w_in[:, pts[i]:pts[i + 1]]
    aq, ak, av, ag, bq, bkv, bkr, bg, ma, mb = (seg(i) for i in range(10))
    wt = jnp.concatenate([aq, ak, av, bq, bkr], axis=1).T.astype(BF16)
    wn = jnp.concatenate([bkv, ag, bg, ma, mb], axis=1).astype(BF16)

    qd = B_NOPE + B_ROPE
    bqup = b_q_up.reshape(B_Q_RANK, B_HEADS, qd)
    bqup = jnp.pad(bqup, ((0, 0), (0, 0), (0, QK_PAD - qd))).reshape(B_Q_RANK, B_HEADS * QK_PAD)

    kvup = b_kv_up.reshape(B_KV_RANK, B_HEADS, B_NOPE + B_V)
    k_part = jnp.pad(kvup[:, :, :B_NOPE], ((0, 0), (0, 0), (0, QK_PAD - B_NOPE)))
    eye = jnp.eye(LANES, QK_PAD, k=B_NOPE, dtype=F32)[:, None, :]
    eye = jnp.where(jnp.arange(LANES)[:, None, None] <= B_ROPE, eye, 0.0)
    r_part = jnp.broadcast_to(eye, (LANES, B_HEADS, QK_PAD))
    waug = jnp.concatenate([k_part, r_part], axis=0).reshape(B_KV_RANK + LANES, B_HEADS * QK_PAD)
    bvup = kvup[:, :, B_NOPE:].reshape(B_KV_RANK, B_HEADS * B_V).T

    return {
        "pre_g": pre_g.reshape(1, D_MODEL),
        "wt": wt, "wn": wn,
        "aqg": a_q_g.reshape(A_HEAD_DIM, 1), "akg": a_k_g.reshape(A_HEAD_DIM, 1),
        "bqg": b_q_g.reshape(B_Q_RANK, 1), "bkvg": b_kv_g.reshape(1, B_KV_RANK),
        "bqup": bqup.T.astype(BF16), "waug": waug.astype(BF16), "bvup": bvup.astype(BF16),
    }


def _layer(x, mod3, pw, post_g, a_out, b_out, w_o):
    n = x.shape[1]
    tables = _rope_tables_t(n, A_HEAD_DIM) + _rope_tables_t(n, B_ROPE)
    qa, ka, va, ag, bg, ma, mb, qb, kb, vb, kna, knb, qna, qnb = _proj_call(x, mod3, pw, tables)
    ya = _attn_call(qa, ka, va, qna, kna, heads=A_HEADS, q_rows=A_HEAD_DIM, q_feat=A_HEAD_DIM,
                    group=A_HEADS // A_KV_HEADS, name="attn_gqa")
    yb = _attn_call(qb, kb, vb, qnb, knb, heads=B_HEADS, q_rows=QK_PAD, q_feat=B_NOPE + B_ROPE,
                    group=1, name="attn_mla")
    return _out_call(x, mod3, post_g, ya, yb, ag, bg, ma, mb, a_out, b_out, w_o)


def kernel(x_prompt, x_sample, c_prompt, c_sample, ada_w, ada_b, pre_norm_g, post_norm_g, w_in,
           a_q_norm_g, a_k_norm_g, b_q_norm_g, b_q_up, b_kv_norm_g, b_kv_up, a_out, b_out, w_o):
    assert ada_w.shape[0] == 1, "single layer"
    bp, bs = c_prompt.shape[0], c_sample.shape[0]
    rows = -(-(bp + bs) // 8) * 8
    c_all = jnp.concatenate([c_prompt, c_sample, jnp.zeros((rows - bp - bs, D_MODEL), F32)], axis=0)
    mod = _mod_call(c_all, ada_w[0].astype(BF16), ada_b[0].reshape(1, 3 * D_MODEL))
    mod3 = mod.reshape(rows, 3, D_MODEL)

    pw = _prep_weights(pre_norm_g[0], w_in[0], a_q_norm_g[0], a_k_norm_g[0], b_q_norm_g[0],
                       b_q_up[0], b_kv_norm_g[0], b_kv_up[0])
    post_g = post_norm_g[0].reshape(1, D_MODEL)
    a_o, b_o, w_oo = a_out[0].astype(BF16), b_out[0].astype(BF16), w_o[0].astype(BF16)

    y_prompt = _layer(x_prompt, mod3[:bp], pw, post_g, a_o, b_o, w_oo)
    y_sample = _layer(x_sample, mod3[bp:bp + bs], pw, post_g, a_o, b_o, w_oo)
    return (y_prompt, y_sample)
```

```python
import functools

import jax
import jax.numpy as jnp
from jax import lax
from jax.experimental import pallas as pl
from jax.experimental.pallas import tpu as pltpu

D_MODEL = 1024
GRID_W = 64
ROPE_THETA = 10000.0
EPS = 1e-6

A_HEADS = 8
A_KV_HEADS = 2
A_HEAD_DIM = 64
A_WIDTH = A_HEADS * A_HEAD_DIM

B_HEADS = 8
B_NOPE = 64
B_ROPE = 32
B_V = 64
B_Q_RANK = 384
B_KV_RANK = 256
B_WIDTH = B_HEADS * B_V

IN_SIZES = (A_WIDTH, A_KV_HEADS * A_HEAD_DIM, A_KV_HEADS * A_HEAD_DIM, A_WIDTH,
            B_Q_RANK, B_KV_RANK, B_ROPE, B_WIDTH, D_MODEL, D_MODEL)

LOG2E = 1.4426950408889634
LANES = 128
QK_PAD = 128
V_ROWS = 64
SHIFT_ROWS = 16
A_QROWS = A_HEAD_DIM + SHIFT_ROWS
PROJ_TILE = 512
V_CHUNK = 256
OUT_TILE = 512
ATTN_TQ = 512
ATTN_TK = 256
ATTN_DEPTH = 2
ATTN_LAG = 2
ATTN_STAGE_BUDGET = 64
ATTN_UNROLL = 60
SAFE_SCORE_BOUND = 60.0
BOUND_SLACK = 1.0625
VMEM_LIMIT = 56 * 1024 * 1024

F32 = jnp.float32
BF16 = jnp.bfloat16


def _rope_tables_t(n, d_rot):
    rows = n // GRID_W
    row_ids = jnp.repeat(jnp.arange(rows, dtype=F32), GRID_W)
    col_ids = jnp.tile(jnp.arange(GRID_W, dtype=F32), rows)
    d_axis = d_rot // 2
    inv = ROPE_THETA ** (-jnp.arange(0, d_axis, 2, dtype=F32) / d_axis)
    ang = jnp.concatenate([row_ids[:, None] * inv, col_ids[:, None] * inv], axis=-1)
    return jnp.cos(ang).T, jnp.sin(ang).T


def _const_spec(shape):
    nd = len(shape)
    return pl.BlockSpec(shape, lambda *_: (0,) * nd, pipeline_mode=pl.Buffered(1))


def _mod_kernel(c_ref, w_ref, b_ref, o_ref):
    c = c_ref[...]
    sc = (c * jax.nn.sigmoid(c)).astype(BF16)
    o_ref[...] = jnp.dot(sc, w_ref[...], preferred_element_type=F32) + b_ref[...]


def _mod_call(c_pad, ada_w, ada_b):
    rows = c_pad.shape[0]
    return pl.pallas_call(
        _mod_kernel,
        out_shape=jax.ShapeDtypeStruct((rows, 3 * D_MODEL), F32),
        name="adaln_mod",
    )(c_pad, ada_w, ada_b)


def _rope_t(x, cos, sin):
    half = x.shape[0] // 2
    x1, x2 = x[:half], x[half:]
    return jnp.concatenate([x1 * cos - x2 * sin, x2 * cos + x1 * sin], axis=0)


def _rms_t(x, g_col):
    ms = jnp.mean(x * x, axis=0, keepdims=True)
    return x * lax.rsqrt(ms + EPS) * g_col


def _proj_kernel(x_ref, mod_ref, pre_g_ref, wt_ref, wn_ref, aqg_ref, akg_ref, bqg_ref, bkvg_ref,
                 bqup_ref, waug_ref, bvup_ref, cos_a_ref, sin_a_ref, cos_b_ref, sin_b_ref,
                 qa_ref, ka_ref, va_ref, ag_ref, bg_ref, ma_ref, mb_ref, qb_ref, kb_ref, vb_ref,
                 kna_ref, knb_ref, qna_ref, qnb_ref):
    t = x_ref.shape[1]

    def tile_max_sq(x_t):
        sq = jnp.sum(x_t * x_t, axis=0, keepdims=True)
        return jnp.broadcast_to(jnp.max(sq, axis=1, keepdims=True), (8, LANES))

    first_row = lax.broadcasted_iota(jnp.int32, (SHIFT_ROWS, t), 0) == 0

    def norm_rows(x_t):
        return jnp.where(first_row, jnp.sqrt(jnp.sum(x_t * x_t, axis=0, keepdims=True)), 0.0)

    one_row = (lax.broadcasted_iota(jnp.int32, (8, t), 0) == 0).astype(F32)
    x = x_ref[0]
    shift = mod_ref[0, 0:1, :]
    scale = mod_ref[0, 1:2, :]
    ms = jnp.mean(x * x, axis=-1, keepdims=True)
    h = (x * lax.rsqrt(ms + EPS) * pre_g_ref[...]) * (1.0 + scale) + shift
    hb = h.astype(BF16)

    t1 = lax.dot_general(wt_ref[...], hb, (((1,), (1,)), ((), ())), preferred_element_type=F32)
    cos_a, sin_a = cos_a_ref[...], sin_a_ref[...]
    cos_b, sin_b = cos_b_ref[...], sin_b_ref[...]
    aqg, akg = aqg_ref[...], akg_ref[...]

    q_scale = (A_HEAD_DIM ** -0.5) * LOG2E
    for hd in range(A_HEADS):
        q = t1[hd * A_HEAD_DIM:(hd + 1) * A_HEAD_DIM]
        q = _rope_t(_rms_t(q, aqg), cos_a, sin_a) * q_scale
        qa_ref[0, hd * A_QROWS:hd * A_QROWS + A_HEAD_DIM, :] = q.astype(BF16)
        qa_ref[0, hd * A_QROWS + A_HEAD_DIM:(hd + 1) * A_QROWS, :] = norm_rows(q).astype(BF16)
        qna_ref[0, hd, 0] = tile_max_sq(q)
    off = A_WIDTH
    for g in range(A_KV_HEADS):
        k = t1[off + g * A_HEAD_DIM:off + (g + 1) * A_HEAD_DIM]
        k = _rope_t(_rms_t(k, akg), cos_a, sin_a)
        slab = jnp.concatenate([k, one_row, jnp.zeros((QK_PAD - A_HEAD_DIM - 8, t), F32)], axis=0)
        ka_ref[0, g] = slab.T.astype(BF16)
        kna_ref[0, g, 0] = tile_max_sq(k)
    off += A_KV_HEADS * A_HEAD_DIM
    for g in range(A_KV_HEADS):
        v = t1[off + g * A_HEAD_DIM:off + (g + 1) * A_HEAD_DIM].astype(BF16)
        for c in range(t // V_CHUNK):
            va_ref[0, g, c] = v[:, c * V_CHUNK:(c + 1) * V_CHUNK]
    off += A_KV_HEADS * A_HEAD_DIM

    bqn = _rms_t(t1[off:off + B_Q_RANK], bqg_ref[...]).astype(BF16)
    off += B_Q_RANK
    qb = jnp.dot(bqup_ref[...], bqn, preferred_element_type=F32)
    qb_scale = ((B_NOPE + B_ROPE) ** -0.5) * LOG2E
    for hd in range(B_HEADS):
        blk = qb[hd * QK_PAD:(hd + 1) * QK_PAD]
        roped = _rope_t(blk[B_NOPE:B_NOPE + B_ROPE], cos_b, sin_b)
        full = jnp.concatenate([blk[:B_NOPE], roped], axis=0) * qb_scale
        pad = jnp.zeros((QK_PAD - B_NOPE - B_ROPE - SHIFT_ROWS, t), F32)
        qb_ref[0, hd * QK_PAD:(hd + 1) * QK_PAD, :] = jnp.concatenate(
            [full, norm_rows(full), pad], axis=0).astype(BF16)
        qnb_ref[0, hd, 0] = tile_max_sq(full)

    kr = _rope_t(t1[off:off + B_ROPE], cos_b, sin_b)
    kr_slab = jnp.concatenate([kr, one_row, jnp.zeros((LANES - B_ROPE - 8, t), F32)], axis=0).T

    bkv = jnp.dot(hb, wn_ref[:, 0:B_KV_RANK], preferred_element_type=F32)
    ms = jnp.mean(bkv * bkv, axis=-1, keepdims=True)
    bkvn = bkv * lax.rsqrt(ms + EPS) * bkvg_ref[...]
    lhs = jnp.concatenate([bkvn, kr_slab], axis=1).astype(BF16)
    kb = jnp.dot(lhs, waug_ref[...], preferred_element_type=F32)
    for hd in range(B_HEADS):
        blk = kb[:, hd * QK_PAD:(hd + 1) * QK_PAD]
        kb_ref[0, hd] = blk.astype(BF16)
        ksq = jnp.sum(blk * blk, axis=1, keepdims=True) - 1.0
        knb_ref[0, hd, 0] = jnp.broadcast_to(jnp.max(ksq, axis=0, keepdims=True), (8, LANES))
    vb = lax.dot_general(bvup_ref[...], bkvn.astype(BF16), (((1,), (1,)), ((), ())),
                         preferred_element_type=F32)
    for hd in range(B_HEADS):
        v = vb[hd * B_V:(hd + 1) * B_V].astype(BF16)
        for c in range(t // V_CHUNK):
            vb_ref[0, hd, c] = v[:, c * V_CHUNK:(c + 1) * V_CHUNK]

    c0 = B_KV_RANK
    ag = jnp.dot(hb, wn_ref[:, c0:c0 + A_WIDTH], preferred_element_type=F32)
    ag_ref[0] = (ag * jax.nn.sigmoid(ag)).astype(BF16)
    c0 += A_WIDTH
    bg = jnp.dot(hb, wn_ref[:, c0:c0 + B_WIDTH], preferred_element_type=F32)
    bg_ref[0] = (bg * jax.nn.sigmoid(bg)).astype(BF16)
    c0 += B_WIDTH
    ma = jnp.dot(hb, wn_ref[:, c0:c0 + D_MODEL], preferred_element_type=F32)
    ma_ref[0] = jax.nn.sigmoid(ma).astype(BF16)
    c0 += D_MODEL
    mb = jnp.dot(hb, wn_ref[:, c0:c0 + D_MODEL], preferred_element_type=F32)
    mb_ref[0] = jax.nn.sigmoid(mb).astype(BF16)


def _proj_call(x, mod3, pw, tables):
    bsz, n, _ = x.shape
    t = PROJ_TILE
    nt = n // t
    cos_a, sin_a, cos_b, sin_b = tables
    tok = lambda b, i: (b, i, 0)
    feat = lambda b, i: (b, 0, i)
    in_specs = [
        pl.BlockSpec((1, t, D_MODEL), tok),
        pl.BlockSpec((1, 3, D_MODEL), lambda b, i: (b, 0, 0)),
        _const_spec((1, D_MODEL)),
        _const_spec(pw["wt"].shape),
        _const_spec(pw["wn"].shape),
        _const_spec((A_HEAD_DIM, 1)),
        _const_spec((A_HEAD_DIM, 1)),
        _const_spec((B_Q_RANK, 1)),
        _const_spec((1, B_KV_RANK)),
        _const_spec(pw["bqup"].shape),
        _const_spec(pw["waug"].shape),
        _const_spec(pw["bvup"].shape),
        pl.BlockSpec((A_HEAD_DIM // 2, t), lambda b, i: (0, i)),
        pl.BlockSpec((A_HEAD_DIM // 2, t), lambda b, i: (0, i)),
        pl.BlockSpec((B_ROPE // 2, t), lambda b, i: (0, i)),
        pl.BlockSpec((B_ROPE // 2, t), lambda b, i: (0, i)),
    ]
    out_shape = [
        jax.ShapeDtypeStruct((bsz, A_HEADS * A_QROWS, n), BF16),
        jax.ShapeDtypeStruct((bsz, A_KV_HEADS, n, QK_PAD), BF16),
        jax.ShapeDtypeStruct((bsz, A_KV_HEADS, n // V_CHUNK, V_ROWS, V_CHUNK), BF16),
        jax.ShapeDtypeStruct((bsz, n, A_WIDTH), BF16),
        jax.ShapeDtypeStruct((bsz, n, B_WIDTH), BF16),
        jax.ShapeDtypeStruct((bsz, n, D_MODEL), BF16),
        jax.ShapeDtypeStruct((bsz, n, D_MODEL), BF16),
        jax.ShapeDtypeStruct((bsz, B_HEADS * QK_PAD, n), BF16),
        jax.ShapeDtypeStruct((bsz, B_HEADS, n, QK_PAD), BF16),
        jax.ShapeDtypeStruct((bsz, B_HEADS, n // V_CHUNK, V_ROWS, V_CHUNK), BF16),
        jax.ShapeDtypeStruct((bsz, A_KV_HEADS, nt, 8, LANES), F32),
        jax.ShapeDtypeStruct((bsz, B_HEADS, nt, 8, LANES), F32),
        jax.ShapeDtypeStruct((bsz, A_HEADS, nt, 8, LANES), F32),
        jax.ShapeDtypeStruct((bsz, B_HEADS, nt, 8, LANES), F32),
    ]
    out_specs = [
        pl.BlockSpec((1, A_HEADS * A_QROWS, t), feat),
        pl.BlockSpec((1, A_KV_HEADS, t, QK_PAD), lambda b, i: (b, 0, i, 0)),
        pl.BlockSpec((1, A_KV_HEADS, t // V_CHUNK, V_ROWS, V_CHUNK), lambda b, i: (b, 0, i, 0, 0)),
        pl.BlockSpec((1, t, A_WIDTH), tok),
        pl.BlockSpec((1, t, B_WIDTH), tok),
        pl.BlockSpec((1, t, D_MODEL), tok),
        pl.BlockSpec((1, t, D_MODEL), tok),
        pl.BlockSpec((1, B_HEADS * QK_PAD, t), feat),
        pl.BlockSpec((1, B_HEADS, t, QK_PAD), lambda b, i: (b, 0, i, 0)),
        pl.BlockSpec((1, B_HEADS, t // V_CHUNK, V_ROWS, V_CHUNK), lambda b, i: (b, 0, i, 0, 0)),
        pl.BlockSpec((1, A_KV_HEADS, 1, 8, LANES), lambda b, i: (b, 0, i, 0, 0)),
        pl.BlockSpec((1, B_HEADS, 1, 8, LANES), lambda b, i: (b, 0, i, 0, 0)),
        pl.BlockSpec((1, A_HEADS, 1, 8, LANES), lambda b, i: (b, 0, i, 0, 0)),
        pl.BlockSpec((1, B_HEADS, 1, 8, LANES), lambda b, i: (b, 0, i, 0, 0)),
    ]
    return pl.pallas_call(
        _proj_kernel,
        grid=(bsz, nt),
        in_specs=in_specs,
        out_specs=out_specs,
        out_shape=out_shape,
        compiler_params=pltpu.CompilerParams(
            dimension_semantics=("arbitrary", "arbitrary"), vmem_limit_bytes=VMEM_LIMIT),
        name="in_proj",
    )(x, mod3, pw["pre_g"], pw["wt"], pw["wn"], pw["aqg"], pw["akg"], pw["bqg"], pw["bkvg"],
      pw["bqup"], pw["waug"], pw["bvup"], cos_a, sin_a, cos_b, sin_b)


def _attn_kernel(small_ref, q_ref, k_ref, v_ref, kn_ref, o_ref, *s_refs, n_steps, q_feat, heads, streams):
    depth = ATTN_DEPTH
    tq = ATTN_TQ
    sub = ATTN_TK // V_CHUNK
    tail_rows = QK_PAD - q_feat - SHIFT_ROWS
    scores_are_small = small_ref[pl.program_id(0) * heads + pl.program_id(1)] == 1
    kmax = jnp.sqrt(kn_ref[0, 0, 0:1, 0:1]) * BOUND_SLACK

    def query_tile(r, shift_scale):
        q_in = q_ref[0, :, r * tq:(r + 1) * tq]
        rows = q_in[q_feat:q_feat + SHIFT_ROWS]
        if shift_scale is None:
            rows = jnp.zeros_like(rows)
        else:
            rows = (shift_scale * rows.astype(F32)).astype(BF16)
        parts = [q_in[:q_feat], rows]
        if tail_rows:
            parts.append(jnp.zeros((tail_rows, tq), BF16))
        return jnp.concatenate(parts, axis=0)

    def key_chunk(j):
        off = pl.multiple_of(j * ATTN_TK, ATTN_TK)
        return k_ref[0, 0, pl.ds(off, ATTN_TK), :]

    def value_chunk(j):
        return jnp.concatenate([v_ref[0, 0, j * sub + c] for c in range(sub)], axis=1)

    def column_sums(p):
        return jnp.sum(p.reshape(ATTN_TK // 8, 8, tq), axis=0)

    def finish(r, acc, lsum):
        o_ref[0, :, r * tq:(r + 1) * tq] = acc / jnp.sum(lsum, axis=0, keepdims=True)

    acc0 = jnp.zeros((V_ROWS, tq), F32)
    lsum0 = jnp.zeros((8, tq), F32)

    @pl.when(scores_are_small)
    def _():
        tiles = ATTN_LAG + 1

        for r in range(streams):
            q_aug = query_tile(r, -kmax)
            base = (r % 2) * tiles if streams > 1 else 0

            def scores(j, slot):
                s_refs[base + slot % tiles][...] = jnp.dot(
                    key_chunk(j), q_aug, preferred_element_type=F32)

            def stages(j0, slot0, carry, count, n_scored):
                acc, lsum = carry
                for u in range(count):
                    if u < n_scored:
                        scores(j0 + u + ATTN_LAG, slot0 + u + ATTN_LAG)
                    p = jnp.exp2(s_refs[base + (slot0 + u) % tiles][...])
                    acc = acc + jnp.dot(value_chunk(j0 + u), p.astype(BF16), preferred_element_type=F32)
                    lsum = lsum + column_sums(p)
                return acc, lsum

            for j in range(ATTN_LAG):
                scores(j, j)
            trips = (n_steps - ATTN_LAG) // ATTN_UNROLL
            carry = lax.fori_loop(
                0, trips, lambda jj, c: stages(jj * ATTN_UNROLL, 0, c, ATTN_UNROLL, ATTN_UNROLL),
                (acc0, lsum0))
            done = trips * ATTN_UNROLL
            rest = n_steps - done
            finish(r, *stages(done, done, carry, rest, rest - ATTN_LAG))

    @pl.when(jnp.logical_not(scores_are_small))
    def _():
        for r in range(streams):
            q_pad = query_tile(r, None)

            def scores(j, s_ref):
                s = jnp.dot(key_chunk(j), q_pad, preferred_element_type=F32)
                s_ref[...] = s
                return jnp.max(s, axis=0, keepdims=True)

            def accumulate(j, s_ref, m, cmax, acc, lsum):
                m_new = jnp.maximum(m, cmax)
                alpha = jnp.exp2(m - m_new)
                p = jnp.exp2(s_ref[...] - m_new)
                acc = alpha * acc + jnp.dot(value_chunk(j), p.astype(BF16), preferred_element_type=F32)
                return m_new, acc, alpha * lsum + column_sums(p)

            def body(jj, carry):
                m, cmax, acc, lsum = carry
                for u in range(depth):
                    j = jj * depth + u
                    cmax_next = scores(jnp.minimum(j + 1, n_steps - 1), s_refs[(u + 1) % depth])
                    m, acc, lsum = accumulate(j, s_refs[u], m, cmax, acc, lsum)
                    cmax = cmax_next
                return m, cmax, acc, lsum

            m0 = jnp.full((1, tq), -1e30, F32)
            cmax0 = scores(0, s_refs[0])
            _, _, acc, lsum = lax.fori_loop(0, n_steps // depth, body, (m0, cmax0, acc0, lsum0))
            finish(r, acc, lsum)


def _attn_call(q, k, v, qn_tiles, kn_tiles, *, heads, q_rows, q_feat, group, name):
    bsz, _, n = q.shape
    n_chunks = v.shape[2]
    n_steps = n // ATTN_TK
    assert n_steps >= ATTN_LAG and n_steps % ATTN_DEPTH == 0 and ATTN_UNROLL % (ATTN_LAG + 1) == 0
    streams = max(1, min(n // ATTN_TQ, ATTN_STAGE_BUDGET // n_steps))
    assert (n // ATTN_TQ) % streams == 0
    tq = ATTN_TQ * streams
    kn = jnp.max(kn_tiles, axis=2)
    qmax2 = jnp.max(qn_tiles, axis=(2, 3, 4))
    kmax2 = jnp.repeat(jnp.max(kn, axis=(2, 3)), group, axis=1)
    small = (qmax2 * kmax2 * BOUND_SLACK ** 2 <= SAFE_SCORE_BOUND ** 2).astype(jnp.int32).reshape(-1)
    kernel = functools.partial(_attn_kernel, n_steps=n_steps, q_feat=q_feat, heads=heads, streams=streams)
    grid_spec = pltpu.PrefetchScalarGridSpec(
        num_scalar_prefetch=1,
        grid=(bsz, heads, n // tq),
        in_specs=[
            pl.BlockSpec((1, q_rows, tq), lambda b, h, i, _: (b, h, i)),
            pl.BlockSpec((1, 1, n, QK_PAD), lambda b, h, i, _: (b, h // group, 0, 0)),
            pl.BlockSpec((1, 1, n_chunks, V_ROWS, V_CHUNK), lambda b, h, i, _: (b, h // group, 0, 0, 0)),
            pl.BlockSpec((1, 1, 8, LANES), lambda b, h, i, _: (b, h // group, 0, 0)),
        ],
        out_specs=pl.BlockSpec((1, V_ROWS, tq), lambda b, h, i, _: (b, h, i)),
        scratch_shapes=[pltpu.VMEM((ATTN_TK, ATTN_TQ), F32)] * (min(streams, 2) * (ATTN_LAG + 1)),
    )
    return pl.pallas_call(
        kernel,
        grid_spec=grid_spec,
        out_shape=jax.ShapeDtypeStruct((bsz, heads * V_ROWS, n), F32),
        compiler_params=pltpu.CompilerParams(
            dimension_semantics=("arbitrary", "arbitrary", "arbitrary"), vmem_limit_bytes=VMEM_LIMIT),
        name=name,
    )(small, q, k, v, kn)


def _out_kernel(x_ref, mod_ref, post_g_ref, ya_ref, yb_ref, ag_ref, bg_ref, ma_ref, mb_ref,
                aout_ref, bout_ref, wo_ref, y_ref):
    ga = (ya_ref[0].T * ag_ref[0].astype(F32)).astype(BF16)
    gb = (yb_ref[0].T * bg_ref[0].astype(F32)).astype(BF16)
    pa = jnp.dot(ga, aout_ref[...], preferred_element_type=F32)
    pb = jnp.dot(gb, bout_ref[...], preferred_element_type=F32)
    merged = ma_ref[0].astype(F32) * pa + mb_ref[0].astype(F32) * pb
    z = jnp.dot(merged.astype(BF16), wo_ref[...], preferred_element_type=F32)
    ms = jnp.mean(z * z, axis=-1, keepdims=True)
    zn = z * lax.rsqrt(ms + EPS) * post_g_ref[...]
    y_ref[0] = x_ref[0] + mod_ref[0, 2:3, :] * zn


def _out_call(x, mod3, post_g, ya, yb, ag, bg, ma, mb, a_out, b_out, w_o):
    bsz, n, _ = x.shape
    t = OUT_TILE
    tok = lambda b, i: (b, i, 0)
    feat = lambda b, i: (b, 0, i)
    return pl.pallas_call(
        _out_kernel,
        grid=(bsz, n // t),
        in_specs=[
            pl.BlockSpec((1, t, D_MODEL), tok),
            pl.BlockSpec((1, 3, D_MODEL), lambda b, i: (b, 0, 0)),
            _const_spec((1, D_MODEL)),
            pl.BlockSpec((1, A_WIDTH, t), feat),
            pl.BlockSpec((1, B_WIDTH, t), feat),
            pl.BlockSpec((1, t, A_WIDTH), tok),
            pl.BlockSpec((1, t, B_WIDTH), tok),
            pl.BlockSpec((1, t, D_MODEL), tok),
            pl.BlockSpec((1, t, D_MODEL), tok),
            _const_spec(a_out.shape),
            _const_spec(b_out.shape),
            _const_spec(w_o.shape),
        ],
        out_specs=pl.BlockSpec((1, t, D_MODEL), tok),
        out_shape=jax.ShapeDtypeStruct((bsz, n, D_MODEL), F32),
        compiler_params=pltpu.CompilerParams(
            dimension_semantics=("arbitrary", "arbitrary"), vmem_limit_bytes=VMEM_LIMIT),
        name="out_proj",
    )(x, mod3, post_g, ya, yb, ag, bg, ma, mb, a_out, b_out, w_o)


def _prep_weights(pre_g, w_in, a_q_g, a_k_g, b_q_g, b_q_up, b_kv_g, b_kv_up):
    pts = [0]
    for s in IN_SIZES:
        pts.append(pts[-1] + s)
    seg = lambda i: w_in[:, pts[i]:pts[i + 1]]
    aq, ak, av, ag, bq, bkv, bkr, bg, ma, mb = (seg(i) for i in range(10))
    wt = jnp.concatenate([aq, ak, av, bq, bkr], axis=1).T.astype(BF16)
    wn = jnp.concatenate([bkv, ag, bg, ma, mb], axis=1).astype(BF16)

    qd = B_NOPE + B_ROPE
    bqup = b_q_up.reshape(B_Q_RANK, B_HEADS, qd)
    bqup = jnp.pad(bqup, ((0, 0), (0, 0), (0, QK_PAD - qd))).reshape(B_Q_RANK, B_HEADS * QK_PAD)

    kvup = b_kv_up.reshape(B_KV_RANK, B_HEADS, B_NOPE + B_V)
    k_part = jnp.pad(kvup[:, :, :B_NOPE], ((0, 0), (0, 0), (0, QK_PAD - B_NOPE)))
    eye = jnp.eye(LANES, QK_PAD, k=B_NOPE, dtype=F32)[:, None, :]
    eye = jnp.where(jnp.arange(LANES)[:, None, None] <= B_ROPE, eye, 0.0)
    r_part = jnp.broadcast_to(eye, (LANES, B_HEADS, QK_PAD))
    waug = jnp.concatenate([k_part, r_part], axis=0).reshape(B_KV_RANK + LANES, B_HEADS * QK_PAD)
    bvup = kvup[:, :, B_NOPE:].reshape(B_KV_RANK, B_HEADS * B_V).T

    return {
        "pre_g": pre_g.reshape(1, D_MODEL),
        "wt": wt, "wn": wn,
        "aqg": a_q_g.reshape(A_HEAD_DIM, 1), "akg": a_k_g.reshape(A_HEAD_DIM, 1),
        "bqg": b_q_g.reshape(B_Q_RANK, 1), "bkvg": b_kv_g.reshape(1, B_KV_RANK),
        "bqup": bqup.T.astype(BF16), "waug": waug.astype(BF16), "bvup": bvup.astype(BF16),
    }


def _layer(x, mod3, pw, tables, post_g, a_out, b_out, w_o):
    qa, ka, va, ag, bg, ma, mb, qb, kb, vb, kna, knb, qna, qnb = _proj_call(x, mod3, pw, tables)
    ya = _attn_call(qa, ka, va, qna, kna, heads=A_HEADS, q_rows=A_QROWS, q_feat=A_HEAD_DIM,
                    group=A_HEADS // A_KV_HEADS, name="attn_gqa")
    yb = _attn_call(qb, kb, vb, qnb, knb, heads=B_HEADS, q_rows=QK_PAD, q_feat=B_NOPE + B_ROPE,
                    group=1, name="attn_mla")
    return _out_call(x, mod3, post_g, ya, yb, ag, bg, ma, mb, a_out, b_out, w_o)


def kernel(x_prompt, x_sample, c_prompt, c_sample, ada_w, ada_b, pre_norm_g, post_norm_g, w_in,
           a_q_norm_g, a_k_norm_g, b_q_norm_g, b_q_up, b_kv_norm_g, b_kv_up, a_out, b_out, w_o):
    assert ada_w.shape[0] == 1, "single layer"
    bp, bs = c_prompt.shape[0], c_sample.shape[0]
    rows = -(-(bp + bs) // 8) * 8
    c_all = jnp.concatenate([c_prompt, c_sample, jnp.zeros((rows - bp - bs, D_MODEL), F32)], axis=0)
    mod = _mod_call(c_all, ada_w[0].astype(BF16), ada_b[0].reshape(1, 3 * D_MODEL))
    mod3 = mod.reshape(rows, 3, D_MODEL)

    pw = _prep_weights(pre_norm_g[0], w_in[0], a_q_norm_g[0], a_k_norm_g[0], b_q_norm_g[0],
                       b_q_up[0], b_kv_norm_g[0], b_kv_up[0])
    post_g = post_norm_g[0].reshape(1, D_MODEL)
    a_o, b_o, w_oo = a_out[0].astype(BF16), b_out[0].astype(BF16), w_o[0].astype(BF16)

    n_max = max(x_prompt.shape[1], x_sample.shape[1])
    tables = _rope_tables_t(n_max, A_HEAD_DIM) + _rope_tables_t(n_max, B_ROPE)
    y_prompt = _layer(x_prompt, mod3[:bp], pw, tables, post_g, a_o, b_o, w_oo)
    y_sample = _layer(x_sample, mod3[bp:bp + bs], pw, tables, post_g, a_o, b_o, w_oo)
    return (y_prompt, y_sample)
```

```python
import functools

import jax
import jax.numpy as jnp
from jax import lax
from jax.experimental import pallas as pl
from jax.experimental.pallas import tpu as pltpu

D_MODEL = 1024
GRID_W = 64
ROPE_THETA = 10000.0
EPS = 1e-6

A_HEADS = 8
A_KV_HEADS = 2
A_HEAD_DIM = 64
A_WIDTH = A_HEADS * A_HEAD_DIM

B_HEADS = 8
B_NOPE = 64
B_ROPE = 32
B_V = 64
B_Q_RANK = 384
B_KV_RANK = 256
B_WIDTH = B_HEADS * B_V

IN_SIZES = (A_WIDTH, A_KV_HEADS * A_HEAD_DIM, A_KV_HEADS * A_HEAD_DIM, A_WIDTH,
            B_Q_RANK, B_KV_RANK, B_ROPE, B_WIDTH, D_MODEL, D_MODEL)

LOG2E = 1.4426950408889634
LANES = 128
QK_PAD = 128
V_ROWS = 64
SHIFT_ROWS = 16
A_QROWS = A_HEAD_DIM + SHIFT_ROWS
PROJ_TILE = 512
V_CHUNK = 256
OUT_TILE = 512
ATTN_TQ = 512
ATTN_TK = 256
ATTN_DEPTH = 2
ATTN_LAG = 2
ATTN_STAGE_BUDGET = 128
ATTN_MAX_TILES = 4
ATTN_UNROLL = 60
SAFE_SCORE_BOUND = 60.0
BOUND_SLACK = 1.0625
VMEM_LIMIT = 56 * 1024 * 1024

F32 = jnp.float32
BF16 = jnp.bfloat16


def _rope_tables_t(n, d_rot):
    rows = n // GRID_W
    row_ids = jnp.repeat(jnp.arange(rows, dtype=F32), GRID_W)
    col_ids = jnp.tile(jnp.arange(GRID_W, dtype=F32), rows)
    d_axis = d_rot // 2
    inv = ROPE_THETA ** (-jnp.arange(0, d_axis, 2, dtype=F32) / d_axis)
    ang = jnp.concatenate([row_ids[:, None] * inv, col_ids[:, None] * inv], axis=-1)
    return jnp.cos(ang).T, jnp.sin(ang).T


def _const_spec(shape):
    nd = len(shape)
    return pl.BlockSpec(shape, lambda *_: (0,) * nd, pipeline_mode=pl.Buffered(1))


def _mod_kernel(c_ref, w_ref, b_ref, o_ref):
    c = c_ref[...]
    sc = (c * jax.nn.sigmoid(c)).astype(BF16)
    o_ref[...] = jnp.dot(sc, w_ref[...], preferred_element_type=F32) + b_ref[...]


def _mod_call(c_pad, ada_w, ada_b):
    rows = c_pad.shape[0]
    return pl.pallas_call(
        _mod_kernel,
        out_shape=jax.ShapeDtypeStruct((rows, 3 * D_MODEL), F32),
        name="adaln_mod",
    )(c_pad, ada_w, ada_b)


def _rope_t(x, cos, sin):
    half = x.shape[0] // 2
    x1, x2 = x[:half], x[half:]
    return jnp.concatenate([x1 * cos - x2 * sin, x2 * cos + x1 * sin], axis=0)


def _rms_t(x, g_col):
    ms = jnp.mean(x * x, axis=0, keepdims=True)
    return x * lax.rsqrt(ms + EPS) * g_col


def _proj_kernel(x_ref, mod_ref, pre_g_ref, wt_ref, wn_ref, aqg_ref, akg_ref, bqg_ref, bkvg_ref,
                 bqup_ref, waug_ref, bvup_ref, cos_a_ref, sin_a_ref, cos_b_ref, sin_b_ref,
                 qa_ref, ka_ref, va_ref, ag_ref, bg_ref, ma_ref, mb_ref, qb_ref, kb_ref, vb_ref,
                 kna_ref, knb_ref, qna_ref, qnb_ref):
    t = x_ref.shape[1]

    def tile_max_sq(x_t):
        sq = jnp.sum(x_t * x_t, axis=0, keepdims=True)
        return jnp.broadcast_to(jnp.max(sq, axis=1, keepdims=True), (8, LANES))

    first_row = lax.broadcasted_iota(jnp.int32, (SHIFT_ROWS, t), 0) == 0

    def norm_rows(x_t):
        return jnp.where(first_row, jnp.sqrt(jnp.sum(x_t * x_t, axis=0, keepdims=True)), 0.0)

    one_row = (lax.broadcasted_iota(jnp.int32, (8, t), 0) == 0).astype(F32)
    x = x_ref[0]
    shift = mod_ref[0, 0:1, :]
    scale = mod_ref[0, 1:2, :]
    ms = jnp.mean(x * x, axis=-1, keepdims=True)
    h = (x * lax.rsqrt(ms + EPS) * pre_g_ref[...]) * (1.0 + scale) + shift
    hb = h.astype(BF16)

    t1 = lax.dot_general(wt_ref[...], hb, (((1,), (1,)), ((), ())), preferred_element_type=F32)
    cos_a, sin_a = cos_a_ref[...], sin_a_ref[...]
    cos_b, sin_b = cos_b_ref[...], sin_b_ref[...]
    aqg, akg = aqg_ref[...], akg_ref[...]

    q_scale = (A_HEAD_DIM ** -0.5) * LOG2E
    for hd in range(A_HEADS):
        q = t1[hd * A_HEAD_DIM:(hd + 1) * A_HEAD_DIM]
        q = _rope_t(_rms_t(q, aqg), cos_a, sin_a) * q_scale
        qa_ref[0, hd * A_QROWS:hd * A_QROWS + A_HEAD_DIM, :] = q.astype(BF16)
        qa_ref[0, hd * A_QROWS + A_HEAD_DIM:(hd + 1) * A_QROWS, :] = norm_rows(q).astype(BF16)
        qna_ref[0, hd, 0] = tile_max_sq(q)
    off = A_WIDTH
    for g in range(A_KV_HEADS):
        k = t1[off + g * A_HEAD_DIM:off + (g + 1) * A_HEAD_DIM]
        k = _rope_t(_rms_t(k, akg), cos_a, sin_a)
        slab = jnp.concatenate([k, one_row, jnp.zeros((QK_PAD - A_HEAD_DIM - 8, t), F32)], axis=0)
        ka_ref[0, g] = slab.T.astype(BF16)
        kna_ref[0, g, 0] = tile_max_sq(k)
    off += A_KV_HEADS * A_HEAD_DIM
    for g in range(A_KV_HEADS):
        v = t1[off + g * A_HEAD_DIM:off + (g + 1) * A_HEAD_DIM].astype(BF16)
        for c in range(t // V_CHUNK):
            va_ref[0, g, c] = v[:, c * V_CHUNK:(c + 1) * V_CHUNK]
    off += A_KV_HEADS * A_HEAD_DIM

    bqn = _rms_t(t1[off:off + B_Q_RANK], bqg_ref[...]).astype(BF16)
    off += B_Q_RANK
    qb = jnp.dot(bqup_ref[...], bqn, preferred_element_type=F32)
    qb_scale = ((B_NOPE + B_ROPE) ** -0.5) * LOG2E
    for hd in range(B_HEADS):
        blk = qb[hd * QK_PAD:(hd + 1) * QK_PAD]
        roped = _rope_t(blk[B_NOPE:B_NOPE + B_ROPE], cos_b, sin_b)
        full = jnp.concatenate([blk[:B_NOPE], roped], axis=0) * qb_scale
        pad = jnp.zeros((QK_PAD - B_NOPE - B_ROPE - SHIFT_ROWS, t), F32)
        qb_ref[0, hd * QK_PAD:(hd + 1) * QK_PAD, :] = jnp.concatenate(
            [full, norm_rows(full), pad], axis=0).astype(BF16)
        qnb_ref[0, hd, 0] = tile_max_sq(full)

    kr = _rope_t(t1[off:off + B_ROPE], cos_b, sin_b)
    kr_slab = jnp.concatenate([kr, one_row, jnp.zeros((LANES - B_ROPE - 8, t), F32)], axis=0).T

    bkv = jnp.dot(hb, wn_ref[:, 0:B_KV_RANK], preferred_element_type=F32)
    ms = jnp.mean(bkv * bkv, axis=-1, keepdims=True)
    bkvn = bkv * lax.rsqrt(ms + EPS) * bkvg_ref[...]
    lhs = jnp.concatenate([bkvn, kr_slab], axis=1).astype(BF16)
    kb = jnp.dot(lhs, waug_ref[...], preferred_element_type=F32)
    for hd in range(B_HEADS):
        blk = kb[:, hd * QK_PAD:(hd + 1) * QK_PAD]
        kb_ref[0, hd] = blk.astype(BF16)
        ksq = jnp.sum(blk * blk, axis=1, keepdims=True) - 1.0
        knb_ref[0, hd, 0] = jnp.broadcast_to(jnp.max(ksq, axis=0, keepdims=True), (8, LANES))
    vb = lax.dot_general(bvup_ref[...], bkvn.astype(BF16), (((1,), (1,)), ((), ())),
                         preferred_element_type=F32)
    for hd in range(B_HEADS):
        v = vb[hd * B_V:(hd + 1) * B_V].astype(BF16)
        for c in range(t // V_CHUNK):
            vb_ref[0, hd, c] = v[:, c * V_CHUNK:(c + 1) * V_CHUNK]

    c0 = B_KV_RANK
    ag = jnp.dot(hb, wn_ref[:, c0:c0 + A_WIDTH], preferred_element_type=F32)
    ag_ref[0] = (ag * jax.nn.sigmoid(ag)).astype(BF16)
    c0 += A_WIDTH
    bg = jnp.dot(hb, wn_ref[:, c0:c0 + B_WIDTH], preferred_element_type=F32)
    bg_ref[0] = (bg * jax.nn.sigmoid(bg)).astype(BF16)
    c0 += B_WIDTH
    ma = jnp.dot(hb, wn_ref[:, c0:c0 + D_MODEL], preferred_element_type=F32)
    ma_ref[0] = jax.nn.sigmoid(ma).astype(BF16)
    c0 += D_MODEL
    mb = jnp.dot(hb, wn_ref[:, c0:c0 + D_MODEL], preferred_element_type=F32)
    mb_ref[0] = jax.nn.sigmoid(mb).astype(BF16)


def _proj_call(x, mod3, pw, tables):
    bsz, n, _ = x.shape
    t = PROJ_TILE
    nt = n // t
    cos_a, sin_a, cos_b, sin_b = tables
    tok = lambda b, i: (b, i, 0)
    feat = lambda b, i: (b, 0, i)
    in_specs = [
        pl.BlockSpec((1, t, D_MODEL), tok),
        pl.BlockSpec((1, 3, D_MODEL), lambda b, i: (b, 0, 0)),
        _const_spec((1, D_MODEL)),
        _const_spec(pw["wt"].shape),
        _const_spec(pw["wn"].shape),
        _const_spec((A_HEAD_DIM, 1)),
        _const_spec((A_HEAD_DIM, 1)),
        _const_spec((B_Q_RANK, 1)),
        _const_spec((1, B_KV_RANK)),
        _const_spec(pw["bqup"].shape),
        _const_spec(pw["waug"].shape),
        _const_spec(pw["bvup"].shape),
        pl.BlockSpec((A_HEAD_DIM // 2, t), lambda b, i: (0, i)),
        pl.BlockSpec((A_HEAD_DIM // 2, t), lambda b, i: (0, i)),
        pl.BlockSpec((B_ROPE // 2, t), lambda b, i: (0, i)),
        pl.BlockSpec((B_ROPE // 2, t), lambda b, i: (0, i)),
    ]
    out_shape = [
        jax.ShapeDtypeStruct((bsz, A_HEADS * A_QROWS, n), BF16),
        jax.ShapeDtypeStruct((bsz, A_KV_HEADS, n, QK_PAD), BF16),
        jax.ShapeDtypeStruct((bsz, A_KV_HEADS, n // V_CHUNK, V_ROWS, V_CHUNK), BF16),
        jax.ShapeDtypeStruct((bsz, n, A_WIDTH), BF16),
        jax.ShapeDtypeStruct((bsz, n, B_WIDTH), BF16),
        jax.ShapeDtypeStruct((bsz, n, D_MODEL), BF16),
        jax.ShapeDtypeStruct((bsz, n, D_MODEL), BF16),
        jax.ShapeDtypeStruct((bsz, B_HEADS * QK_PAD, n), BF16),
        jax.ShapeDtypeStruct((bsz, B_HEADS, n, QK_PAD), BF16),
        jax.ShapeDtypeStruct((bsz, B_HEADS, n // V_CHUNK, V_ROWS, V_CHUNK), BF16),
        jax.ShapeDtypeStruct((bsz, A_KV_HEADS, nt, 8, LANES), F32),
        jax.ShapeDtypeStruct((bsz, B_HEADS, nt, 8, LANES), F32),
        jax.ShapeDtypeStruct((bsz, A_HEADS, nt, 8, LANES), F32),
        jax.ShapeDtypeStruct((bsz, B_HEADS, nt, 8, LANES), F32),
    ]
    out_specs = [
        pl.BlockSpec((1, A_HEADS * A_QROWS, t), feat),
        pl.BlockSpec((1, A_KV_HEADS, t, QK_PAD), lambda b, i: (b, 0, i, 0)),
        pl.BlockSpec((1, A_KV_HEADS, t // V_CHUNK, V_ROWS, V_CHUNK), lambda b, i: (b, 0, i, 0, 0)),
        pl.BlockSpec((1, t, A_WIDTH), tok),
        pl.BlockSpec((1, t, B_WIDTH), tok),
        pl.BlockSpec((1, t, D_MODEL), tok),
        pl.BlockSpec((1, t, D_MODEL), tok),
        pl.BlockSpec((1, B_HEADS * QK_PAD, t), feat),
        pl.BlockSpec((1, B_HEADS, t, QK_PAD), lambda b, i: (b, 0, i, 0)),
        pl.BlockSpec((1, B_HEADS, t // V_CHUNK, V_ROWS, V_CHUNK), lambda b, i: (b, 0, i, 0, 0)),
        pl.BlockSpec((1, A_KV_HEADS, 1, 8, LANES), lambda b, i: (b, 0, i, 0, 0)),
        pl.BlockSpec((1, B_HEADS, 1, 8, LANES), lambda b, i: (b, 0, i, 0, 0)),
        pl.BlockSpec((1, A_HEADS, 1, 8, LANES), lambda b, i: (b, 0, i, 0, 0)),
        pl.BlockSpec((1, B_HEADS, 1, 8, LANES), lambda b, i: (b, 0, i, 0, 0)),
    ]
    return pl.pallas_call(
        _proj_kernel,
        grid=(bsz, nt),
        in_specs=in_specs,
        out_specs=out_specs,
        out_shape=out_shape,
        compiler_params=pltpu.CompilerParams(
            dimension_semantics=("arbitrary", "arbitrary"), vmem_limit_bytes=VMEM_LIMIT),
        name="in_proj",
    )(x, mod3, pw["pre_g"], pw["wt"], pw["wn"], pw["aqg"], pw["akg"], pw["bqg"], pw["bkvg"],
      pw["bqup"], pw["waug"], pw["bvup"], cos_a, sin_a, cos_b, sin_b)


def _attn_kernel(small_ref, q_ref, k_ref, v_ref, kn_ref, o_ref, *s_refs, n_steps, q_feat, heads, streams):
    depth = ATTN_DEPTH
    tq = ATTN_TQ
    sub = ATTN_TK // V_CHUNK
    tail_rows = QK_PAD - q_feat - SHIFT_ROWS
    scores_are_small = small_ref[pl.program_id(0) * heads + pl.program_id(1)] == 1
    kmax = jnp.sqrt(kn_ref[0, 0, 0:1, 0:1]) * BOUND_SLACK

    def query_tile(r, shift_scale):
        q_in = q_ref[0, :, r * tq:(r + 1) * tq]
        rows = q_in[q_feat:q_feat + SHIFT_ROWS]
        if shift_scale is None:
            rows = jnp.zeros_like(rows)
        else:
            rows = (shift_scale * rows.astype(F32)).astype(BF16)
        parts = [q_in[:q_feat], rows]
        if tail_rows:
            parts.append(jnp.zeros((tail_rows, tq), BF16))
        return jnp.concatenate(parts, axis=0)

    def key_chunk(j):
        off = pl.multiple_of(j * ATTN_TK, ATTN_TK)
        return k_ref[0, 0, pl.ds(off, ATTN_TK), :]

    def value_chunk(j):
        return jnp.concatenate([v_ref[0, 0, j * sub + c] for c in range(sub)], axis=1)

    def column_sums(p):
        return jnp.sum(p.reshape(ATTN_TK // 8, 8, tq), axis=0)

    def finish(r, acc, lsum):
        o_ref[0, :, r * tq:(r + 1) * tq] = (acc / jnp.sum(lsum, axis=0, keepdims=True)).astype(o_ref.dtype)

    acc0 = jnp.zeros((V_ROWS, tq), F32)
    lsum0 = jnp.zeros((8, tq), F32)

    @pl.when(scores_are_small)
    def _():
        tiles = ATTN_LAG + 1

        for r in range(streams):
            q_aug = query_tile(r, -kmax)
            base = (r % 2) * tiles if streams > 1 else 0

            def scores(j, slot):
                s_refs[base + slot % tiles][...] = jnp.dot(
                    key_chunk(j), q_aug, preferred_element_type=F32)

            def stages(j0, slot0, carry, count, n_scored):
                acc, lsum = carry
                for u in range(count):
                    if u < n_scored:
                        scores(j0 + u + ATTN_LAG, slot0 + u + ATTN_LAG)
                    p = jnp.exp2(s_refs[base + (slot0 + u) % tiles][...])
                    acc = acc + jnp.dot(value_chunk(j0 + u), p.astype(BF16), preferred_element_type=F32)
                    lsum = lsum + column_sums(p)
                return acc, lsum

            for j in range(ATTN_LAG):
                scores(j, j)
            trips = (n_steps - ATTN_LAG) // ATTN_UNROLL
            carry = lax.fori_loop(
                0, trips, lambda jj, c: stages(jj * ATTN_UNROLL, 0, c, ATTN_UNROLL, ATTN_UNROLL),
                (acc0, lsum0))
            done = trips * ATTN_UNROLL
            rest = n_steps - done
            finish(r, *stages(done, done, carry, rest, rest - ATTN_LAG))

    @pl.when(jnp.logical_not(scores_are_small))
    def _():
        for r in range(streams):
            q_pad = query_tile(r, None)

            def scores(j, s_ref):
                s = jnp.dot(key_chunk(j), q_pad, preferred_element_type=F32)
                s_ref[...] = s
                return jnp.max(s, axis=0, keepdims=True)

            def accumulate(j, s_ref, m, cmax, acc, lsum):
                m_new = jnp.maximum(m, cmax)
                alpha = jnp.exp2(m - m_new)
                p = jnp.exp2(s_ref[...] - m_new)
                acc = alpha * acc + jnp.dot(value_chunk(j), p.astype(BF16), preferred_element_type=F32)
                return m_new, acc, alpha * lsum + column_sums(p)

            def body(jj, carry):
                m, cmax, acc, lsum = carry
                for u in range(depth):
                    j = jj * depth + u
                    cmax_next = scores(jnp.minimum(j + 1, n_steps - 1), s_refs[(u + 1) % depth])
                    m, acc, lsum = accumulate(j, s_refs[u], m, cmax, acc, lsum)
                    cmax = cmax_next
                return m, cmax, acc, lsum

            m0 = jnp.full((1, tq), -1e30, F32)
            cmax0 = scores(0, s_refs[0])
            _, _, acc, lsum = lax.fori_loop(0, n_steps // depth, body, (m0, cmax0, acc0, lsum0))
            finish(r, acc, lsum)


def _attn_call(q, k, v, qn_tiles, kn_tiles, *, heads, q_rows, q_feat, group, name):
    bsz, _, n = q.shape
    n_chunks = v.shape[2]
    n_steps = n // ATTN_TK
    assert n_steps >= ATTN_LAG and n_steps % ATTN_DEPTH == 0 and ATTN_UNROLL % (ATTN_LAG + 1) == 0
    streams = max(1, min(n // ATTN_TQ, ATTN_STAGE_BUDGET // n_steps, ATTN_MAX_TILES))
    assert (n // ATTN_TQ) % streams == 0
    tq = ATTN_TQ * streams
    kn = jnp.max(kn_tiles, axis=2)
    qmax2 = jnp.max(qn_tiles, axis=(2, 3, 4))
    kmax2 = jnp.repeat(jnp.max(kn, axis=(2, 3)), group, axis=1)
    small = (qmax2 * kmax2 * BOUND_SLACK ** 2 <= SAFE_SCORE_BOUND ** 2).astype(jnp.int32).reshape(-1)
    kernel = functools.partial(_attn_kernel, n_steps=n_steps, q_feat=q_feat, heads=heads, streams=streams)
    grid_spec = pltpu.PrefetchScalarGridSpec(
        num_scalar_prefetch=1,
        grid=(bsz, heads, n // tq),
        in_specs=[
            pl.BlockSpec((1, q_rows, tq), lambda b, h, i, _: (b, h, i)),
            pl.BlockSpec((1, 1, n, QK_PAD), lambda b, h, i, _: (b, h // group, 0, 0)),
            pl.BlockSpec((1, 1, n_chunks, V_ROWS, V_CHUNK), lambda b, h, i, _: (b, h // group, 0, 0, 0)),
            pl.BlockSpec((1, 1, 8, LANES), lambda b, h, i, _: (b, h // group, 0, 0)),
        ],
        out_specs=pl.BlockSpec((1, V_ROWS, tq), lambda b, h, i, _: (b, h, i)),
        scratch_shapes=[pltpu.VMEM((ATTN_TK, ATTN_TQ), F32)] * (min(streams, 2) * (ATTN_LAG + 1)),
    )
    return pl.pallas_call(
        kernel,
        grid_spec=grid_spec,
        out_shape=jax.ShapeDtypeStruct((bsz, heads * V_ROWS, n), BF16),
        compiler_params=pltpu.CompilerParams(
            dimension_semantics=("arbitrary", "arbitrary", "arbitrary"), vmem_limit_bytes=VMEM_LIMIT),
        name=name,
    )(small, q, k, v, kn)


def _out_kernel(x_ref, mod_ref, post_g_ref, ya_ref, yb_ref, ag_ref, bg_ref, ma_ref, mb_ref,
                aout_ref, bout_ref, wo_ref, y_ref):
    ga = (ya_ref[0].astype(F32).T * ag_ref[0].astype(F32)).astype(BF16)
    gb = (yb_ref[0].astype(F32).T * bg_ref[0].astype(F32)).astype(BF16)
    pa = jnp.dot(ga, aout_ref[...], preferred_element_type=F32)
    pb = jnp.dot(gb, bout_ref[...], preferred_element_type=F32)
    merged = ma_ref[0].astype(F32) * pa + mb_ref[0].astype(F32) * pb
    z = jnp.dot(merged.astype(BF16), wo_ref[...], preferred_element_type=F32)
    ms = jnp.mean(z * z, axis=-1, keepdims=True)
    zn = z * lax.rsqrt(ms + EPS) * post_g_ref[...]
    y_ref[0] = x_ref[0] + mod_ref[0, 2:3, :] * zn


def _out_call(x, mod3, post_g, ya, yb, ag, bg, ma, mb, a_out, b_out, w_o):
    bsz, n, _ = x.shape
    t = OUT_TILE
    tok = lambda b, i: (b, i, 0)
    feat = lambda b, i: (b, 0, i)
    return pl.pallas_call(
        _out_kernel,
        grid=(bsz, n // t),
        in_specs=[
            pl.BlockSpec((1, t, D_MODEL), tok),
            pl.BlockSpec((1, 3, D_MODEL), lambda b, i: (b, 0, 0)),
            _const_spec((1, D_MODEL)),
            pl.BlockSpec((1, A_WIDTH, t), feat),
            pl.BlockSpec((1, B_WIDTH, t), feat),
            pl.BlockSpec((1, t, A_WIDTH), tok),
            pl.BlockSpec((1, t, B_WIDTH), tok),
            pl.BlockSpec((1, t, D_MODEL), tok),
            pl.BlockSpec((1, t, D_MODEL), tok),
            _const_spec(a_out.shape),
            _const_spec(b_out.shape),
            _const_spec(w_o.shape),
        ],
        out_specs=pl.BlockSpec((1, t, D_MODEL), tok),
        out_shape=jax.ShapeDtypeStruct((bsz, n, D_MODEL), F32),
        compiler_params=pltpu.CompilerParams(
            dimension_semantics=("arbitrary", "arbitrary"), vmem_limit_bytes=VMEM_LIMIT),
        name="out_proj",
    )(x, mod3, post_g, ya, yb, ag, bg, ma, mb, a_out, b_out, w_o)


def _prep_weights(pre_g, w_in, a_q_g, a_k_g, b_q_g, b_q_up, b_kv_g, b_kv_up):
    pts = [0]
    for s in IN_SIZES:
        pts.append(pts[-1] + s)
    seg = lambda i: w_in[:, pts[i]:pts[i + 1]]
    aq, ak, av, ag, bq, bkv, bkr, bg, ma, mb = (seg(i) for i in range(10))
    wt = jnp.concatenate([aq, ak, av, bq, bkr], axis=1).T.astype(BF16)
    wn = jnp.concatenate([bkv, ag, bg, ma, mb], axis=1).astype(BF16)

    qd = B_NOPE + B_ROPE
    bqup = b_q_up.reshape(B_Q_RANK, B_HEADS, qd)
    bqup = jnp.pad(bqup, ((0, 0), (0, 0), (0, QK_PAD - qd))).reshape(B_Q_RANK, B_HEADS * QK_PAD)

    kvup = b_kv_up.reshape(B_KV_RANK, B_HEADS, B_NOPE + B_V)
    k_part = jnp.pad(kvup[:, :, :B_NOPE], ((0, 0), (0, 0), (0, QK_PAD - B_NOPE)))
    eye = jnp.eye(LANES, QK_PAD, k=B_NOPE, dtype=F32)[:, None, :]
    eye = jnp.where(jnp.arange(LANES)[:, None, None] <= B_ROPE, eye, 0.0)
    r_part = jnp.broadcast_to(eye, (LANES, B_HEADS, QK_PAD))
    waug = jnp.concatenate([k_part, r_part], axis=0).reshape(B_KV_RANK + LANES, B_HEADS * QK_PAD)
    bvup = kvup[:, :, B_NOPE:].reshape(B_KV_RANK, B_HEADS * B_V).T

    return {
        "pre_g": pre_g.reshape(1, D_MODEL),
        "wt": wt, "wn": wn,
        "aqg": a_q_g.reshape(A_HEAD_DIM, 1), "akg": a_k_g.reshape(A_HEAD_DIM, 1),
        "bqg": b_q_g.reshape(B_Q_RANK, 1), "bkvg": b_kv_g.reshape(1, B_KV_RANK),
        "bqup": bqup.T.astype(BF16), "waug": waug.astype(BF16), "bvup": bvup.astype(BF16),
    }


def _layer(x, mod3, pw, tables, post_g, a_out, b_out, w_o):
    qa, ka, va, ag, bg, ma, mb, qb, kb, vb, kna, knb, qna, qnb = _proj_call(x, mod3, pw, tables)
    ya = _attn_call(qa, ka, va, qna, kna, heads=A_HEADS, q_rows=A_QROWS, q_feat=A_HEAD_DIM,
                    group=A_HEADS // A_KV_HEADS, name="attn_gqa")
    yb = _attn_call(qb, kb, vb, qnb, knb, heads=B_HEADS, q_rows=QK_PAD, q_feat=B_NOPE + B_ROPE,
                    group=1, name="attn_mla")
    return _out_call(x, mod3, post_g, ya, yb, ag, bg, ma, mb, a_out, b_out, w_o)


def kernel(x_prompt, x_sample, c_prompt, c_sample, ada_w, ada_b, pre_norm_g, post_norm_g, w_in,
           a_q_norm_g, a_k_norm_g, b_q_norm_g, b_q_up, b_kv_norm_g, b_kv_up, a_out, b_out, w_o):
    assert ada_w.shape[0] == 1, "single layer"
    bp, bs = c_prompt.shape[0], c_sample.shape[0]
    rows = -(-(bp + bs) // 8) * 8
    c_all = jnp.concatenate([c_prompt, c_sample, jnp.zeros((rows - bp - bs, D_MODEL), F32)], axis=0)
    mod = _mod_call(c_all, ada_w[0].astype(BF16), ada_b[0].reshape(1, 3 * D_MODEL))
    mod3 = mod.reshape(rows, 3, D_MODEL)

    pw = _prep_weights(pre_norm_g[0], w_in[0], a_q_norm_g[0], a_k_norm_g[0], b_q_norm_g[0],
                       b_q_up[0], b_kv_norm_g[0], b_kv_up[0])
    post_g = post_norm_g[0].reshape(1, D_MODEL)
    a_o, b_o, w_oo = a_out[0].astype(BF16), b_out[0].astype(BF16), w_o[0].astype(BF16)

    n_max = max(x_prompt.shape[1], x_sample.shape[1])
    tables = _rope_tables_t(n_max, A_HEAD_DIM) + _rope_tables_t(n_max, B_ROPE)
    y_prompt = _layer(x_prompt, mod3[:bp], pw, tables, post_g, a_o, b_o, w_oo)
    y_sample = _layer(x_sample, mod3[bp:bp + bs], pw, tables, post_g, a_o, b_o, w_oo)
    return (y_prompt, y_sample)
```

```python
import functools

import jax
import jax.numpy as jnp
from jax import lax
from jax.experimental import pallas as pl
from jax.experimental.pallas import tpu as pltpu

D_MODEL = 1024
GRID_W = 64
ROPE_THETA = 10000.0
EPS = 1e-6

A_HEADS = 8
A_KV_HEADS = 2
A_HEAD_DIM = 64
A_WIDTH = A_HEADS * A_HEAD_DIM

B_HEADS = 8
B_NOPE = 64
B_ROPE = 32
B_V = 64
B_Q_RANK = 384
B_KV_RANK = 256
B_WIDTH = B_HEADS * B_V

IN_SIZES = (A_WIDTH, A_KV_HEADS * A_HEAD_DIM, A_KV_HEADS * A_HEAD_DIM, A_WIDTH,
            B_Q_RANK, B_KV_RANK, B_ROPE, B_WIDTH, D_MODEL, D_MODEL)

LOG2E = 1.4426950408889634
LANES = 128
QK_PAD = 128
V_ROWS = 64
SHIFT_ROWS = 16
A_QROWS = A_HEAD_DIM + SHIFT_ROWS
PROJ_TILE = 1024
V_CHUNK = 256
OUT_TILE = 512
ATTN_TQ = 512
ATTN_TK = 256
ATTN_DEPTH = 2
ATTN_LAG = 2
ATTN_STAGE_BUDGET = 128
ATTN_MAX_TILES = 8
ATTN_UNROLL = 60
SAFE_SCORE_BOUND = 60.0
BOUND_SLACK = 1.0625
VMEM_LIMIT = 56 * 1024 * 1024

F32 = jnp.float32
BF16 = jnp.bfloat16


def _rope_tables_t(n, d_rot):
    rows = n // GRID_W
    row_ids = jnp.repeat(jnp.arange(rows, dtype=F32), GRID_W)
    col_ids = jnp.tile(jnp.arange(GRID_W, dtype=F32), rows)
    d_axis = d_rot // 2
    inv = ROPE_THETA ** (-jnp.arange(0, d_axis, 2, dtype=F32) / d_axis)
    ang = jnp.concatenate([row_ids[:, None] * inv, col_ids[:, None] * inv], axis=-1)
    return jnp.cos(ang).T, jnp.sin(ang).T


def _const_spec(shape):
    nd = len(shape)
    return pl.BlockSpec(shape, lambda *_: (0,) * nd, pipeline_mode=pl.Buffered(1))


def _mod_kernel(c_ref, w_ref, b_ref, o_ref):
    c = c_ref[...]
    sc = (c * jax.nn.sigmoid(c)).astype(BF16)
    o_ref[...] = jnp.dot(sc, w_ref[...], preferred_element_type=F32) + b_ref[...]


def _mod_call(c_pad, ada_w, ada_b):
    rows = c_pad.shape[0]
    return pl.pallas_call(
        _mod_kernel,
        out_shape=jax.ShapeDtypeStruct((rows, 3 * D_MODEL), F32),
        name="adaln_mod",
    )(c_pad, ada_w, ada_b)


def _rope_t(x, cos, sin):
    half = x.shape[0] // 2
    x1, x2 = x[:half], x[half:]
    return jnp.concatenate([x1 * cos - x2 * sin, x2 * cos + x1 * sin], axis=0)


def _rms_t(x, g_col):
    ms = jnp.mean(x * x, axis=0, keepdims=True)
    return x * lax.rsqrt(ms + EPS) * g_col


def _proj_kernel(x_ref, mod_ref, pre_g_ref, wt_ref, wn_ref, aqg_ref, akg_ref, bqg_ref, bkvg_ref,
                 bqup_ref, waug_ref, bvup_ref, cos_a_ref, sin_a_ref, cos_b_ref, sin_b_ref,
                 qa_ref, ka_ref, va_ref, ag_ref, bg_ref, ma_ref, mb_ref, qb_ref, kb_ref, vb_ref,
                 kna_ref, knb_ref, qna_ref, qnb_ref):
    t = x_ref.shape[1]

    def tile_max_sq(x_t):
        sq = jnp.sum(x_t * x_t, axis=0, keepdims=True)
        return jnp.broadcast_to(jnp.max(sq, axis=1, keepdims=True), (8, LANES))

    first_row = lax.broadcasted_iota(jnp.int32, (SHIFT_ROWS, t), 0) == 0

    def norm_rows(x_t):
        return jnp.where(first_row, jnp.sqrt(jnp.sum(x_t * x_t, axis=0, keepdims=True)), 0.0)

    one_row = (lax.broadcasted_iota(jnp.int32, (8, t), 0) == 0).astype(F32)
    x = x_ref[0]
    shift = mod_ref[0, 0:1, :]
    scale = mod_ref[0, 1:2, :]
    ms = jnp.mean(x * x, axis=-1, keepdims=True)
    h = (x * lax.rsqrt(ms + EPS) * pre_g_ref[...]) * (1.0 + scale) + shift
    hb = h.astype(BF16)

    t1 = lax.dot_general(wt_ref[...], hb, (((1,), (1,)), ((), ())), preferred_element_type=F32)
    cos_a, sin_a = cos_a_ref[...], sin_a_ref[...]
    cos_b, sin_b = cos_b_ref[...], sin_b_ref[...]
    aqg, akg = aqg_ref[...], akg_ref[...]

    q_scale = (A_HEAD_DIM ** -0.5) * LOG2E
    for hd in range(A_HEADS):
        q = t1[hd * A_HEAD_DIM:(hd + 1) * A_HEAD_DIM]
        q = _rope_t(_rms_t(q, aqg), cos_a, sin_a) * q_scale
        qa_ref[0, hd * A_QROWS:hd * A_QROWS + A_HEAD_DIM, :] = q.astype(BF16)
        qa_ref[0, hd * A_QROWS + A_HEAD_DIM:(hd + 1) * A_QROWS, :] = norm_rows(q).astype(BF16)
        qna_ref[0, hd, 0] = tile_max_sq(q)
    off = A_WIDTH
    for g in range(A_KV_HEADS):
        k = t1[off + g * A_HEAD_DIM:off + (g + 1) * A_HEAD_DIM]
        k = _rope_t(_rms_t(k, akg), cos_a, sin_a)
        slab = jnp.concatenate([k, one_row, jnp.zeros((QK_PAD - A_HEAD_DIM - 8, t), F32)], axis=0)
        ka_ref[0, g] = slab.T.astype(BF16)
        kna_ref[0, g, 0] = tile_max_sq(k)
    off += A_KV_HEADS * A_HEAD_DIM
    for g in range(A_KV_HEADS):
        v = t1[off + g * A_HEAD_DIM:off + (g + 1) * A_HEAD_DIM].astype(BF16)
        for c in range(t // V_CHUNK):
            va_ref[0, g, c] = v[:, c * V_CHUNK:(c + 1) * V_CHUNK]
    off += A_KV_HEADS * A_HEAD_DIM

    bqn = _rms_t(t1[off:off + B_Q_RANK], bqg_ref[...]).astype(BF16)
    off += B_Q_RANK
    qb = jnp.dot(bqup_ref[...], bqn, preferred_element_type=F32)
    qb_scale = ((B_NOPE + B_ROPE) ** -0.5) * LOG2E
    for hd in range(B_HEADS):
        blk = qb[hd * QK_PAD:(hd + 1) * QK_PAD]
        roped = _rope_t(blk[B_NOPE:B_NOPE + B_ROPE], cos_b, sin_b)
        full = jnp.concatenate([blk[:B_NOPE], roped], axis=0) * qb_scale
        pad = jnp.zeros((QK_PAD - B_NOPE - B_ROPE - SHIFT_ROWS, t), F32)
        qb_ref[0, hd * QK_PAD:(hd + 1) * QK_PAD, :] = jnp.concatenate(
            [full, norm_rows(full), pad], axis=0).astype(BF16)
        qnb_ref[0, hd, 0] = tile_max_sq(full)

    kr = _rope_t(t1[off:off + B_ROPE], cos_b, sin_b)
    kr_slab = jnp.concatenate([kr, one_row, jnp.zeros((LANES - B_ROPE - 8, t), F32)], axis=0).T

    bkv = jnp.dot(hb, wn_ref[:, 0:B_KV_RANK], preferred_element_type=F32)
    ms = jnp.mean(bkv * bkv, axis=-1, keepdims=True)
    bkvn = bkv * lax.rsqrt(ms + EPS) * bkvg_ref[...]
    lhs = jnp.concatenate([bkvn, kr_slab], axis=1).astype(BF16)
    kb = jnp.dot(lhs, waug_ref[...], preferred_element_type=F32)
    for hd in range(B_HEADS):
        blk = kb[:, hd * QK_PAD:(hd + 1) * QK_PAD]
        kb_ref[0, hd] = blk.astype(BF16)
        ksq = jnp.sum(blk * blk, axis=1, keepdims=True) - 1.0
        knb_ref[0, hd, 0] = jnp.broadcast_to(jnp.max(ksq, axis=0, keepdims=True), (8, LANES))
    vb = lax.dot_general(bvup_ref[...], bkvn.astype(BF16), (((1,), (1,)), ((), ())),
                         preferred_element_type=F32)
    for hd in range(B_HEADS):
        v = vb[hd * B_V:(hd + 1) * B_V].astype(BF16)
        for c in range(t // V_CHUNK):
            vb_ref[0, hd, c] = v[:, c * V_CHUNK:(c + 1) * V_CHUNK]

    c0 = B_KV_RANK
    ag = jnp.dot(hb, wn_ref[:, c0:c0 + A_WIDTH], preferred_element_type=F32)
    ag_ref[0] = (ag * jax.nn.sigmoid(ag)).astype(BF16)
    c0 += A_WIDTH
    bg = jnp.dot(hb, wn_ref[:, c0:c0 + B_WIDTH], preferred_element_type=F32)
    bg_ref[0] = (bg * jax.nn.sigmoid(bg)).astype(BF16)
    c0 += B_WIDTH
    ma = jnp.dot(hb, wn_ref[:, c0:c0 + D_MODEL], preferred_element_type=F32)
    ma_ref[0] = jax.nn.sigmoid(ma).astype(BF16)
    c0 += D_MODEL
    mb = jnp.dot(hb, wn_ref[:, c0:c0 + D_MODEL], preferred_element_type=F32)
    mb_ref[0] = jax.nn.sigmoid(mb).astype(BF16)


def _proj_call(x, mod3, pw, tables):
    bsz, n, _ = x.shape
    t = PROJ_TILE
    nt = n // t
    cos_a, sin_a, cos_b, sin_b = tables
    tok = lambda b, i: (b, i, 0)
    feat = lambda b, i: (b, 0, i)
    in_specs = [
        pl.BlockSpec((1, t, D_MODEL), tok),
        pl.BlockSpec((1, 3, D_MODEL), lambda b, i: (b, 0, 0)),
        _const_spec((1, D_MODEL)),
        _const_spec(pw["wt"].shape),
        _const_spec(pw["wn"].shape),
        _const_spec((A_HEAD_DIM, 1)),
        _const_spec((A_HEAD_DIM, 1)),
        _const_spec((B_Q_RANK, 1)),
        _const_spec((1, B_KV_RANK)),
        _const_spec(pw["bqup"].shape),
        _const_spec(pw["waug"].shape),
        _const_spec(pw["bvup"].shape),
        pl.BlockSpec((A_HEAD_DIM // 2, t), lambda b, i: (0, i)),
        pl.BlockSpec((A_HEAD_DIM // 2, t), lambda b, i: (0, i)),
        pl.BlockSpec((B_ROPE // 2, t), lambda b, i: (0, i)),
        pl.BlockSpec((B_ROPE // 2, t), lambda b, i: (0, i)),
    ]
    out_shape = [
        jax.ShapeDtypeStruct((bsz, A_HEADS * A_QROWS, n), BF16),
        jax.ShapeDtypeStruct((bsz, A_KV_HEADS, n, QK_PAD), BF16),
        jax.ShapeDtypeStruct((bsz, A_KV_HEADS, n // V_CHUNK, V_ROWS, V_CHUNK), BF16),
        jax.ShapeDtypeStruct((bsz, n, A_WIDTH), BF16),
        jax.ShapeDtypeStruct((bsz, n, B_WIDTH), BF16),
        jax.ShapeDtypeStruct((bsz, n, D_MODEL), BF16),
        jax.ShapeDtypeStruct((bsz, n, D_MODEL), BF16),
        jax.ShapeDtypeStruct((bsz, B_HEADS * QK_PAD, n), BF16),
        jax.ShapeDtypeStruct((bsz, B_HEADS, n, QK_PAD), BF16),
        jax.ShapeDtypeStruct((bsz, B_HEADS, n // V_CHUNK, V_ROWS, V_CHUNK), BF16),
        jax.ShapeDtypeStruct((bsz, A_KV_HEADS, nt, 8, LANES), F32),
        jax.ShapeDtypeStruct((bsz, B_HEADS, nt, 8, LANES), F32),
        jax.ShapeDtypeStruct((bsz, A_HEADS, nt, 8, LANES), F32),
        jax.ShapeDtypeStruct((bsz, B_HEADS, nt, 8, LANES), F32),
    ]
    out_specs = [
        pl.BlockSpec((1, A_HEADS * A_QROWS, t), feat),
        pl.BlockSpec((1, A_KV_HEADS, t, QK_PAD), lambda b, i: (b, 0, i, 0)),
        pl.BlockSpec((1, A_KV_HEADS, t // V_CHUNK, V_ROWS, V_CHUNK), lambda b, i: (b, 0, i, 0, 0)),
        pl.BlockSpec((1, t, A_WIDTH), tok),
        pl.BlockSpec((1, t, B_WIDTH), tok),
        pl.BlockSpec((1, t, D_MODEL), tok),
        pl.BlockSpec((1, t, D_MODEL), tok),
        pl.BlockSpec((1, B_HEADS * QK_PAD, t), feat),
        pl.BlockSpec((1, B_HEADS, t, QK_PAD), lambda b, i: (b, 0, i, 0)),
        pl.BlockSpec((1, B_HEADS, t // V_CHUNK, V_ROWS, V_CHUNK), lambda b, i: (b, 0, i, 0, 0)),
        pl.BlockSpec((1, A_KV_HEADS, 1, 8, LANES), lambda b, i: (b, 0, i, 0, 0)),
        pl.BlockSpec((1, B_HEADS, 1, 8, LANES), lambda b, i: (b, 0, i, 0, 0)),
        pl.BlockSpec((1, A_HEADS, 1, 8, LANES), lambda b, i: (b, 0, i, 0, 0)),
        pl.BlockSpec((1, B_HEADS, 1, 8, LANES), lambda b, i: (b, 0, i, 0, 0)),
    ]
    return pl.pallas_call(
        _proj_kernel,
        grid=(bsz, nt),
        in_specs=in_specs,
        out_specs=out_specs,
        out_shape=out_shape,
        compiler_params=pltpu.CompilerParams(
            dimension_semantics=("arbitrary", "arbitrary"), vmem_limit_bytes=VMEM_LIMIT),
        name="in_proj",
    )(x, mod3, pw["pre_g"], pw["wt"], pw["wn"], pw["aqg"], pw["akg"], pw["bqg"], pw["bkvg"],
      pw["bqup"], pw["waug"], pw["bvup"], cos_a, sin_a, cos_b, sin_b)


def _attn_kernel(small_ref, q_ref, k_ref, v_ref, kn_ref, o_ref, *s_refs, n_steps, q_feat, heads, streams):
    depth = ATTN_DEPTH
    tq = ATTN_TQ
    sub = ATTN_TK // V_CHUNK
    tail_rows = QK_PAD - q_feat - SHIFT_ROWS
    scores_are_small = small_ref[pl.program_id(0) * heads + pl.program_id(1)] == 1
    kmax = jnp.sqrt(kn_ref[0, 0, 0:1, 0:1]) * BOUND_SLACK

    def query_tile(r, shift_scale):
        q_in = q_ref[0, :, r * tq:(r + 1) * tq]
        rows = q_in[q_feat:q_feat + SHIFT_ROWS]
        if shift_scale is None:
            rows = jnp.zeros_like(rows)
        else:
            rows = (shift_scale * rows.astype(F32)).astype(BF16)
        parts = [q_in[:q_feat], rows]
        if tail_rows:
            parts.append(jnp.zeros((tail_rows, tq), BF16))
        return jnp.concatenate(parts, axis=0)

    def key_chunk(j):
        off = pl.multiple_of(j * ATTN_TK, ATTN_TK)
        return k_ref[0, 0, pl.ds(off, ATTN_TK), :]

    def value_chunk(j):
        return jnp.concatenate([v_ref[0, 0, j * sub + c] for c in range(sub)], axis=1)

    def column_sums(p):
        return jnp.sum(p.reshape(ATTN_TK // 8, 8, tq), axis=0)

    def finish(r, acc, lsum):
        o_ref[0, :, r * tq:(r + 1) * tq] = (acc / jnp.sum(lsum, axis=0, keepdims=True)).astype(o_ref.dtype)

    acc0 = jnp.zeros((V_ROWS, tq), F32)
    lsum0 = jnp.zeros((8, tq), F32)

    @pl.when(scores_are_small)
    def _():
        tiles = ATTN_LAG + 1

        for r in range(streams):
            q_aug = query_tile(r, -kmax)
            base = (r % 2) * tiles if streams > 1 else 0

            def scores(j, slot):
                s_refs[base + slot % tiles][...] = jnp.dot(
                    key_chunk(j), q_aug, preferred_element_type=F32)

            def stages(j0, slot0, carry, count, n_scored):
                acc, lsum = carry
                for u in range(count):
                    if u < n_scored:
                        scores(j0 + u + ATTN_LAG, slot0 + u + ATTN_LAG)
                    p = jnp.exp2(s_refs[base + (slot0 + u) % tiles][...])
                    acc = acc + jnp.dot(value_chunk(j0 + u), p.astype(BF16), preferred_element_type=F32)
                    lsum = lsum + column_sums(p)
                return acc, lsum

            for j in range(ATTN_LAG):
                scores(j, j)
            trips = (n_steps - ATTN_LAG) // ATTN_UNROLL
            carry = lax.fori_loop(
                0, trips, lambda jj, c: stages(jj * ATTN_UNROLL, 0, c, ATTN_UNROLL, ATTN_UNROLL),
                (acc0, lsum0))
            done = trips * ATTN_UNROLL
            rest = n_steps - done
            finish(r, *stages(done, done, carry, rest, rest - ATTN_LAG))

    @pl.when(jnp.logical_not(scores_are_small))
    def _():
        for r in range(streams):
            q_pad = query_tile(r, None)

            def scores(j, s_ref):
                s = jnp.dot(key_chunk(j), q_pad, preferred_element_type=F32)
                s_ref[...] = s
                return jnp.max(s, axis=0, keepdims=True)

            def accumulate(j, s_ref, m, cmax, acc, lsum):
                m_new = jnp.maximum(m, cmax)
                alpha = jnp.exp2(m - m_new)
                p = jnp.exp2(s_ref[...] - m_new)
                acc = alpha * acc + jnp.dot(value_chunk(j), p.astype(BF16), preferred_element_type=F32)
                return m_new, acc, alpha * lsum + column_sums(p)

            def body(jj, carry):
                m, cmax, acc, lsum = carry
                for u in range(depth):
                    j = jj * depth + u
                    cmax_next = scores(jnp.minimum(j + 1, n_steps - 1), s_refs[(u + 1) % depth])
                    m, acc, lsum = accumulate(j, s_refs[u], m, cmax, acc, lsum)
                    cmax = cmax_next
                return m, cmax, acc, lsum

            m0 = jnp.full((1, tq), -1e30, F32)
            cmax0 = scores(0, s_refs[0])
            _, _, acc, lsum = lax.fori_loop(0, n_steps // depth, body, (m0, cmax0, acc0, lsum0))
            finish(r, acc, lsum)


def _attn_call(q, k, v, qn_tiles, kn_tiles, *, heads, q_rows, q_feat, group, name):
    bsz, _, n = q.shape
    n_chunks = v.shape[2]
    n_steps = n // ATTN_TK
    assert n_steps >= ATTN_LAG and n_steps % ATTN_DEPTH == 0 and ATTN_UNROLL % (ATTN_LAG + 1) == 0
    streams = max(1, min(n // ATTN_TQ, ATTN_STAGE_BUDGET // n_steps, ATTN_MAX_TILES))
    assert (n // ATTN_TQ) % streams == 0
    tq = ATTN_TQ * streams
    kn = jnp.max(kn_tiles, axis=2)
    qmax2 = jnp.max(qn_tiles, axis=(2, 3, 4))
    kmax2 = jnp.repeat(jnp.max(kn, axis=(2, 3)), group, axis=1)
    small = (qmax2 * kmax2 * BOUND_SLACK ** 2 <= SAFE_SCORE_BOUND ** 2).astype(jnp.int32).reshape(-1)
    kernel = functools.partial(_attn_kernel, n_steps=n_steps, q_feat=q_feat, heads=heads, streams=streams)
    grid_spec = pltpu.PrefetchScalarGridSpec(
        num_scalar_prefetch=1,
        grid=(bsz, heads, n // tq),
        in_specs=[
            pl.BlockSpec((1, q_rows, tq), lambda b, h, i, _: (b, h, i)),
            pl.BlockSpec((1, 1, n, QK_PAD), lambda b, h, i, _: (b, h // group, 0, 0)),
            pl.BlockSpec((1, 1, n_chunks, V_ROWS, V_CHUNK), lambda b, h, i, _: (b, h // group, 0, 0, 0)),
            pl.BlockSpec((1, 1, 8, LANES), lambda b, h, i, _: (b, h // group, 0, 0)),
        ],
        out_specs=pl.BlockSpec((1, V_ROWS, tq), lambda b, h, i, _: (b, h, i)),
        scratch_shapes=[pltpu.VMEM((ATTN_TK, ATTN_TQ), F32)] * (min(streams, 2) * (ATTN_LAG + 1)),
    )
    return pl.pallas_call(
        kernel,
        grid_spec=grid_spec,
        out_shape=jax.ShapeDtypeStruct((bsz, heads * V_ROWS, n), BF16),
        compiler_params=pltpu.CompilerParams(
            dimension_semantics=("arbitrary", "arbitrary", "arbitrary"), vmem_limit_bytes=VMEM_LIMIT),
        name=name,
    )(small, q, k, v, kn)


def _out_kernel(x_ref, mod_ref, post_g_ref, ya_ref, yb_ref, ag_ref, bg_ref, ma_ref, mb_ref,
                aout_ref, bout_ref, wo_ref, y_ref):
    ga = (ya_ref[0].astype(F32).T * ag_ref[0].astype(F32)).astype(BF16)
    gb = (yb_ref[0].astype(F32).T * bg_ref[0].astype(F32)).astype(BF16)
    pa = jnp.dot(ga, aout_ref[...], preferred_element_type=F32)
    pb = jnp.dot(gb, bout_ref[...], preferred_element_type=F32)
    merged = ma_ref[0].astype(F32) * pa + mb_ref[0].astype(F32) * pb
    z = jnp.dot(merged.astype(BF16), wo_ref[...], preferred_element_type=F32)
    ms = jnp.mean(z * z, axis=-1, keepdims=True)
    zn = z * lax.rsqrt(ms + EPS) * post_g_ref[...]
    y_ref[0] = x_ref[0] + mod_ref[0, 2:3, :] * zn


def _out_call(x, mod3, post_g, ya, yb, ag, bg, ma, mb, a_out, b_out, w_o):
    bsz, n, _ = x.shape
    t = OUT_TILE
    tok = lambda b, i: (b, i, 0)
    feat = lambda b, i: (b, 0, i)
    return pl.pallas_call(
        _out_kernel,
        grid=(bsz, n // t),
        in_specs=[
            pl.BlockSpec((1, t, D_MODEL), tok),
            pl.BlockSpec((1, 3, D_MODEL), lambda b, i: (b, 0, 0)),
            _const_spec((1, D_MODEL)),
            pl.BlockSpec((1, A_WIDTH, t), feat),
            pl.BlockSpec((1, B_WIDTH, t), feat),
            pl.BlockSpec((1, t, A_WIDTH), tok),
            pl.BlockSpec((1, t, B_WIDTH), tok),
            pl.BlockSpec((1, t, D_MODEL), tok),
            pl.BlockSpec((1, t, D_MODEL), tok),
            _const_spec(a_out.shape),
            _const_spec(b_out.shape),
            _const_spec(w_o.shape),
        ],
        out_specs=pl.BlockSpec((1, t, D_MODEL), tok),
        out_shape=jax.ShapeDtypeStruct((bsz, n, D_MODEL), F32),
        compiler_params=pltpu.CompilerParams(
            dimension_semantics=("arbitrary", "arbitrary"), vmem_limit_bytes=VMEM_LIMIT),
        name="out_proj",
    )(x, mod3, post_g, ya, yb, ag, bg, ma, mb, a_out, b_out, w_o)


def _prep_weights(pre_g, w_in, a_q_g, a_k_g, b_q_g, b_q_up, b_kv_g, b_kv_up):
    pts = [0]
    for s in IN_SIZES:
        pts.append(pts[-1] + s)
    seg = lambda i: w_in[:, pts[i]:pts[i + 1]]
    aq, ak, av, ag, bq, bkv, bkr, bg, ma, mb = (seg(i) for i in range(10))
    wt = jnp.concatenate([aq, ak, av, bq, bkr], axis=1).T.astype(BF16)
    wn = jnp.concatenate([bkv, ag, bg, ma, mb], axis=1).astype(BF16)

    qd = B_NOPE + B_ROPE
    bqup = b_q_up.reshape(B_Q_RANK, B_HEADS, qd)
    bqup = jnp.pad(bqup, ((0, 0), (0, 0), (0, QK_PAD - qd))).reshape(B_Q_RANK, B_HEADS * QK_PAD)

    kvup = b_kv_up.reshape(B_KV_RANK, B_HEADS, B_NOPE + B_V)
    k_part = jnp.pad(kvup[:, :, :B_NOPE], ((0, 0), (0, 0), (0, QK_PAD - B_NOPE)))
    eye = jnp.eye(LANES, QK_PAD, k=B_NOPE, dtype=F32)[:, None, :]
    eye = jnp.where(jnp.arange(LANES)[:, None, None] <= B_ROPE, eye, 0.0)
    r_part = jnp.broadcast_to(eye, (LANES, B_HEADS, QK_PAD))
    waug = jnp.concatenate([k_part, r_part], axis=0).reshape(B_KV_RANK + LANES, B_HEADS * QK_PAD)
    bvup = kvup[:, :, B_NOPE:].reshape(B_KV_RANK, B_HEADS * B_V).T

    return {
        "pre_g": pre_g.reshape(1, D_MODEL),
        "wt": wt, "wn": wn,
        "aqg": a_q_g.reshape(A_HEAD_DIM, 1), "akg": a_k_g.reshape(A_HEAD_DIM, 1),
        "bqg": b_q_g.reshape(B_Q_RANK, 1), "bkvg": b_kv_g.reshape(1, B_KV_RANK),
        "bqup": bqup.T.astype(BF16), "waug": waug.astype(BF16), "bvup": bvup.astype(BF16),
    }


def _layer(x, mod3, pw, tables, post_g, a_out, b_out, w_o):
    qa, ka, va, ag, bg, ma, mb, qb, kb, vb, kna, knb, qna, qnb = _proj_call(x, mod3, pw, tables)
    ya = _attn_call(qa, ka, va, qna, kna, heads=A_HEADS, q_rows=A_QROWS, q_feat=A_HEAD_DIM,
                    group=A_HEADS // A_KV_HEADS, name="attn_gqa")
    yb = _attn_call(qb, kb, vb, qnb, knb, heads=B_HEADS, q_rows=QK_PAD, q_feat=B_NOPE + B_ROPE,
                    group=1, name="attn_mla")
    return _out_call(x, mod3, post_g, ya, yb, ag, bg, ma, mb, a_out, b_out, w_o)


def kernel(x_prompt, x_sample, c_prompt, c_sample, ada_w, ada_b, pre_norm_g, post_norm_g, w_in,
           a_q_norm_g, a_k_norm_g, b_q_norm_g, b_q_up, b_kv_norm_g, b_kv_up, a_out, b_out, w_o):
    assert ada_w.shape[0] == 1, "single layer"
    bp, bs = c_prompt.shape[0], c_sample.shape[0]
    rows = -(-(bp + bs) // 8) * 8
    c_all = jnp.concatenate([c_prompt, c_sample, jnp.zeros((rows - bp - bs, D_MODEL), F32)], axis=0)
    mod = _mod_call(c_all, ada_w[0].astype(BF16), ada_b[0].reshape(1, 3 * D_MODEL))
    mod3 = mod.reshape(rows, 3, D_MODEL)

    pw = _prep_weights(pre_norm_g[0], w_in[0], a_q_norm_g[0], a_k_norm_g[0], b_q_norm_g[0],
                       b_q_up[0], b_kv_norm_g[0], b_kv_up[0])
    post_g = post_norm_g[0].reshape(1, D_MODEL)
    a_o, b_o, w_oo = a_out[0].astype(BF16), b_out[0].astype(BF16), w_o[0].astype(BF16)

    n_max = max(x_prompt.shape[1], x_sample.shape[1])
    tables = _rope_tables_t(n_max, A_HEAD_DIM) + _rope_tables_t(n_max, B_ROPE)
    y_prompt = _layer(x_prompt, mod3[:bp], pw, tables, post_g, a_o, b_o, w_oo)
    y_sample = _layer(x_sample, mod3[bp:bp + bs], pw, tables, post_g, a_o, b_o, w_oo)
    return (y_prompt, y_sample)
```

```python
import functools

import jax
import jax.numpy as jnp
from jax import lax
from jax.experimental import pallas as pl
from jax.experimental.pallas import tpu as pltpu

D_MODEL = 1024
GRID_W = 64
ROPE_THETA = 10000.0
EPS = 1e-6

A_HEADS = 8
A_KV_HEADS = 2
A_HEAD_DIM = 64
A_WIDTH = A_HEADS * A_HEAD_DIM

B_HEADS = 8
B_NOPE = 64
B_ROPE = 32
B_V = 64
B_Q_RANK = 384
B_KV_RANK = 256
B_WIDTH = B_HEADS * B_V

IN_SIZES = (A_WIDTH, A_KV_HEADS * A_HEAD_DIM, A_KV_HEADS * A_HEAD_DIM, A_WIDTH,
            B_Q_RANK, B_KV_RANK, B_ROPE, B_WIDTH, D_MODEL, D_MODEL)

LOG2E = 1.4426950408889634
LANES = 128
QK_PAD = 128
V_ROWS = 64
SHIFT_ROWS = 16
A_QROWS = A_HEAD_DIM + SHIFT_ROWS
PROJ_TILE = 1024
V_CHUNK = 256
OUT_TILE = 1024
ATTN_TQ = 512
ATTN_TK = 256
ATTN_DEPTH = 2
ATTN_LAG = 2
ATTN_STAGE_BUDGET = 128
ATTN_MAX_TILES = 8
ATTN_UNROLL = 60
SAFE_SCORE_BOUND = 60.0
BOUND_SLACK = 1.0625
VMEM_LIMIT = 56 * 1024 * 1024

F32 = jnp.float32
BF16 = jnp.bfloat16


def _rope_tables_t(n, d_rot):
    rows = n // GRID_W
    row_ids = jnp.repeat(jnp.arange(rows, dtype=F32), GRID_W)
    col_ids = jnp.tile(jnp.arange(GRID_W, dtype=F32), rows)
    d_axis = d_rot // 2
    inv = ROPE_THETA ** (-jnp.arange(0, d_axis, 2, dtype=F32) / d_axis)
    ang = jnp.concatenate([row_ids[:, None] * inv, col_ids[:, None] * inv], axis=-1)
    return jnp.cos(ang).T, jnp.sin(ang).T


def _const_spec(shape):
    nd = len(shape)
    return pl.BlockSpec(shape, lambda *_: (0,) * nd, pipeline_mode=pl.Buffered(1))


def _mod_kernel(c_ref, w_ref, b_ref, o_ref):
    c = c_ref[...]
    sc = (c * jax.nn.sigmoid(c)).astype(BF16)
    o_ref[...] = jnp.dot(sc, w_ref[...], preferred_element_type=F32) + b_ref[...]


def _mod_call(c_pad, ada_w, ada_b):
    rows = c_pad.shape[0]
    return pl.pallas_call(
        _mod_kernel,
        out_shape=jax.ShapeDtypeStruct((rows, 3 * D_MODEL), F32),
        name="adaln_mod",
    )(c_pad, ada_w, ada_b)


def _rope_t(x, cos, sin):
    half = x.shape[0] // 2
    x1, x2 = x[:half], x[half:]
    return jnp.concatenate([x1 * cos - x2 * sin, x2 * cos + x1 * sin], axis=0)


def _rms_t(x, g_col):
    ms = jnp.mean(x * x, axis=0, keepdims=True)
    return x * lax.rsqrt(ms + EPS) * g_col


def _proj_kernel(x_ref, mod_ref, pre_g_ref, wt_ref, wn_ref, aqg_ref, akg_ref, bqg_ref, bkvg_ref,
                 bqup_ref, waug_ref, bvup_ref, cos_a_ref, sin_a_ref, cos_b_ref, sin_b_ref,
                 qa_ref, ka_ref, va_ref, ag_ref, bg_ref, ma_ref, mb_ref, qb_ref, kb_ref, vb_ref,
                 kna_ref, knb_ref, qna_ref, qnb_ref):
    t = x_ref.shape[1]

    def tile_max_sq(x_t):
        sq = jnp.sum(x_t * x_t, axis=0, keepdims=True)
        return jnp.broadcast_to(jnp.max(sq, axis=1, keepdims=True), (8, LANES))

    first_row = lax.broadcasted_iota(jnp.int32, (SHIFT_ROWS, t), 0) == 0

    def norm_rows(x_t):
        return jnp.where(first_row, jnp.sqrt(jnp.sum(x_t * x_t, axis=0, keepdims=True)), 0.0)

    one_row = (lax.broadcasted_iota(jnp.int32, (8, t), 0) == 0).astype(F32)
    x = x_ref[0]
    shift = mod_ref[0, 0:1, :]
    scale = mod_ref[0, 1:2, :]
    ms = jnp.mean(x * x, axis=-1, keepdims=True)
    h = (x * lax.rsqrt(ms + EPS) * pre_g_ref[...]) * (1.0 + scale) + shift
    hb = h.astype(BF16)

    t1 = lax.dot_general(wt_ref[...], hb, (((1,), (1,)), ((), ())), preferred_element_type=F32)
    cos_a, sin_a = cos_a_ref[...], sin_a_ref[...]
    cos_b, sin_b = cos_b_ref[...], sin_b_ref[...]
    aqg, akg = aqg_ref[...], akg_ref[...]

    q_scale = (A_HEAD_DIM ** -0.5) * LOG2E
    for hd in range(A_HEADS):
        q = t1[hd * A_HEAD_DIM:(hd + 1) * A_HEAD_DIM]
        q = _rope_t(_rms_t(q, aqg), cos_a, sin_a) * q_scale
        qa_ref[0, hd * A_QROWS:hd * A_QROWS + A_HEAD_DIM, :] = q.astype(BF16)
        qa_ref[0, hd * A_QROWS + A_HEAD_DIM:(hd + 1) * A_QROWS, :] = norm_rows(q).astype(BF16)
        qna_ref[0, hd, 0] = tile_max_sq(q)
    off = A_WIDTH
    for g in range(A_KV_HEADS):
        k = t1[off + g * A_HEAD_DIM:off + (g + 1) * A_HEAD_DIM]
        k = _rope_t(_rms_t(k, akg), cos_a, sin_a)
        slab = jnp.concatenate([k, one_row, jnp.zeros((QK_PAD - A_HEAD_DIM - 8, t), F32)], axis=0)
        ka_ref[0, g] = slab.T.astype(BF16)
        kna_ref[0, g, 0] = tile_max_sq(k)
    off += A_KV_HEADS * A_HEAD_DIM
    for g in range(A_KV_HEADS):
        v = t1[off + g * A_HEAD_DIM:off + (g + 1) * A_HEAD_DIM].astype(BF16)
        for c in range(t // V_CHUNK):
            va_ref[0, g, c] = v[:, c * V_CHUNK:(c + 1) * V_CHUNK]
    off += A_KV_HEADS * A_HEAD_DIM

    bqn = _rms_t(t1[off:off + B_Q_RANK], bqg_ref[...]).astype(BF16)
    off += B_Q_RANK
    qb = jnp.dot(bqup_ref[...], bqn, preferred_element_type=F32)
    qb_scale = ((B_NOPE + B_ROPE) ** -0.5) * LOG2E
    for hd in range(B_HEADS):
        blk = qb[hd * QK_PAD:(hd + 1) * QK_PAD]
        roped = _rope_t(blk[B_NOPE:B_NOPE + B_ROPE], cos_b, sin_b)
        full = jnp.concatenate([blk[:B_NOPE], roped], axis=0) * qb_scale
        pad = jnp.zeros((QK_PAD - B_NOPE - B_ROPE - SHIFT_ROWS, t), F32)
        qb_ref[0, hd * QK_PAD:(hd + 1) * QK_PAD, :] = jnp.concatenate(
            [full, norm_rows(full), pad], axis=0).astype(BF16)
        qnb_ref[0, hd, 0] = tile_max_sq(full)

    kr = _rope_t(t1[off:off + B_ROPE], cos_b, sin_b)
    kr_slab = jnp.concatenate([kr, one_row, jnp.zeros((LANES - B_ROPE - 8, t), F32)], axis=0).T

    bkv = jnp.dot(hb, wn_ref[:, 0:B_KV_RANK], preferred_element_type=F32)
    ms = jnp.mean(bkv * bkv, axis=-1, keepdims=True)
    bkvn = bkv * lax.rsqrt(ms + EPS) * bkvg_ref[...]
    lhs = jnp.concatenate([bkvn, kr_slab], axis=1).astype(BF16)
    kb = jnp.dot(lhs, waug_ref[...], preferred_element_type=F32)
    for hd in range(B_HEADS):
        blk = kb[:, hd * QK_PAD:(hd + 1) * QK_PAD]
        kb_ref[0, hd] = blk.astype(BF16)
        ksq = jnp.sum(blk * blk, axis=1, keepdims=True) - 1.0
        knb_ref[0, hd, 0] = jnp.broadcast_to(jnp.max(ksq, axis=0, keepdims=True), (8, LANES))
    vb = lax.dot_general(bvup_ref[...], bkvn.astype(BF16), (((1,), (1,)), ((), ())),
                         preferred_element_type=F32)
    for hd in range(B_HEADS):
        v = vb[hd * B_V:(hd + 1) * B_V].astype(BF16)
        for c in range(t // V_CHUNK):
            vb_ref[0, hd, c] = v[:, c * V_CHUNK:(c + 1) * V_CHUNK]

    c0 = B_KV_RANK
    ag = jnp.dot(hb, wn_ref[:, c0:c0 + A_WIDTH], preferred_element_type=F32)
    ag_ref[0] = (ag * jax.nn.sigmoid(ag)).astype(BF16)
    c0 += A_WIDTH
    bg = jnp.dot(hb, wn_ref[:, c0:c0 + B_WIDTH], preferred_element_type=F32)
    bg_ref[0] = (bg * jax.nn.sigmoid(bg)).astype(BF16)
    c0 += B_WIDTH
    ma = jnp.dot(hb, wn_ref[:, c0:c0 + D_MODEL], preferred_element_type=F32)
    ma_ref[0] = jax.nn.sigmoid(ma).astype(BF16)
    c0 += D_MODEL
    mb = jnp.dot(hb, wn_ref[:, c0:c0 + D_MODEL], preferred_element_type=F32)
    mb_ref[0] = jax.nn.sigmoid(mb).astype(BF16)


def _proj_call(x, mod3, pw, tables):
    bsz, n, _ = x.shape
    t = PROJ_TILE
    nt = n // t
    cos_a, sin_a, cos_b, sin_b = tables
    tok = lambda b, i: (b, i, 0)
    feat = lambda b, i: (b, 0, i)
    in_specs = [
        pl.BlockSpec((1, t, D_MODEL), tok),
        pl.BlockSpec((1, 3, D_MODEL), lambda b, i: (b, 0, 0)),
        _const_spec((1, D_MODEL)),
        _const_spec(pw["wt"].shape),
        _const_spec(pw["wn"].shape),
        _const_spec((A_HEAD_DIM, 1)),
        _const_spec((A_HEAD_DIM, 1)),
        _const_spec((B_Q_RANK, 1)),
        _const_spec((1, B_KV_RANK)),
        _const_spec(pw["bqup"].shape),
        _const_spec(pw["waug"].shape),
        _const_spec(pw["bvup"].shape),
        pl.BlockSpec((A_HEAD_DIM // 2, t), lambda b, i: (0, i)),
        pl.BlockSpec((A_HEAD_DIM // 2, t), lambda b, i: (0, i)),
        pl.BlockSpec((B_ROPE // 2, t), lambda b, i: (0, i)),
        pl.BlockSpec((B_ROPE // 2, t), lambda b, i: (0, i)),
    ]
    out_shape = [
        jax.ShapeDtypeStruct((bsz, A_HEADS * A_QROWS, n), BF16),
        jax.ShapeDtypeStruct((bsz, A_KV_HEADS, n, QK_PAD), BF16),
        jax.ShapeDtypeStruct((bsz, A_KV_HEADS, n // V_CHUNK, V_ROWS, V_CHUNK), BF16),
        jax.ShapeDtypeStruct((bsz, n, A_WIDTH), BF16),
        jax.ShapeDtypeStruct((bsz, n, B_WIDTH), BF16),
        jax.ShapeDtypeStruct((bsz, n, D_MODEL), BF16),
        jax.ShapeDtypeStruct((bsz, n, D_MODEL), BF16),
        jax.ShapeDtypeStruct((bsz, B_HEADS * QK_PAD, n), BF16),
        jax.ShapeDtypeStruct((bsz, B_HEADS, n, QK_PAD), BF16),
        jax.ShapeDtypeStruct((bsz, B_HEADS, n // V_CHUNK, V_ROWS, V_CHUNK), BF16),
        jax.ShapeDtypeStruct((bsz, A_KV_HEADS, nt, 8, LANES), F32),
        jax.ShapeDtypeStruct((bsz, B_HEADS, nt, 8, LANES), F32),
        jax.ShapeDtypeStruct((bsz, A_HEADS, nt, 8, LANES), F32),
        jax.ShapeDtypeStruct((bsz, B_HEADS, nt, 8, LANES), F32),
    ]
    out_specs = [
        pl.BlockSpec((1, A_HEADS * A_QROWS, t), feat),
        pl.BlockSpec((1, A_KV_HEADS, t, QK_PAD), lambda b, i: (b, 0, i, 0)),
        pl.BlockSpec((1, A_KV_HEADS, t // V_CHUNK, V_ROWS, V_CHUNK), lambda b, i: (b, 0, i, 0, 0)),
        pl.BlockSpec((1, t, A_WIDTH), tok),
        pl.BlockSpec((1, t, B_WIDTH), tok),
        pl.BlockSpec((1, t, D_MODEL), tok),
        pl.BlockSpec((1, t, D_MODEL), tok),
        pl.BlockSpec((1, B_HEADS * QK_PAD, t), feat),
        pl.BlockSpec((1, B_HEADS, t, QK_PAD), lambda b, i: (b, 0, i, 0)),
        pl.BlockSpec((1, B_HEADS, t // V_CHUNK, V_ROWS, V_CHUNK), lambda b, i: (b, 0, i, 0, 0)),
        pl.BlockSpec((1, A_KV_HEADS, 1, 8, LANES), lambda b, i: (b, 0, i, 0, 0)),
        pl.BlockSpec((1, B_HEADS, 1, 8, LANES), lambda b, i: (b, 0, i, 0, 0)),
        pl.BlockSpec((1, A_HEADS, 1, 8, LANES), lambda b, i: (b, 0, i, 0, 0)),
        pl.BlockSpec((1, B_HEADS, 1, 8, LANES), lambda b, i: (b, 0, i, 0, 0)),
    ]
    return pl.pallas_call(
        _proj_kernel,
        grid=(bsz, nt),
        in_specs=in_specs,
        out_specs=out_specs,
        out_shape=out_shape,
        compiler_params=pltpu.CompilerParams(
            dimension_semantics=("arbitrary", "arbitrary"), vmem_limit_bytes=VMEM_LIMIT),
        name="in_proj",
    )(x, mod3, pw["pre_g"], pw["wt"], pw["wn"], pw["aqg"], pw["akg"], pw["bqg"], pw["bkvg"],
      pw["bqup"], pw["waug"], pw["bvup"], cos_a, sin_a, cos_b, sin_b)


def _attn_kernel(small_ref, q_ref, k_ref, v_ref, kn_ref, o_ref, *s_refs, n_steps, q_feat, heads, streams):
    depth = ATTN_DEPTH
    tq = ATTN_TQ
    sub = ATTN_TK // V_CHUNK
    tail_rows = QK_PAD - q_feat - SHIFT_ROWS
    scores_are_small = small_ref[pl.program_id(0) * heads + pl.program_id(1)] == 1
    kmax = jnp.sqrt(kn_ref[0, 0, 0:1, 0:1]) * BOUND_SLACK

    def query_tile(r, shift_scale):
        q_in = q_ref[0, :, r * tq:(r + 1) * tq]
        rows = q_in[q_feat:q_feat + SHIFT_ROWS]
        if shift_scale is None:
            rows = jnp.zeros_like(rows)
        else:
            rows = (shift_scale * rows.astype(F32)).astype(BF16)
        parts = [q_in[:q_feat], rows]
        if tail_rows:
            parts.append(jnp.zeros((tail_rows, tq), BF16))
        return jnp.concatenate(parts, axis=0)

    def key_chunk(j):
        off = pl.multiple_of(j * ATTN_TK, ATTN_TK)
        return k_ref[0, 0, pl.ds(off, ATTN_TK), :]

    def value_chunk(j):
        return jnp.concatenate([v_ref[0, 0, j * sub + c] for c in range(sub)], axis=1)

    def column_sums(p):
        return jnp.sum(p.reshape(ATTN_TK // 8, 8, tq), axis=0)

    def finish(r, acc, lsum):
        o_ref[0, :, r * tq:(r + 1) * tq] = (acc / jnp.sum(lsum, axis=0, keepdims=True)).astype(o_ref.dtype)

    acc0 = jnp.zeros((V_ROWS, tq), F32)
    lsum0 = jnp.zeros((8, tq), F32)

    @pl.when(scores_are_small)
    def _():
        tiles = ATTN_LAG + 1

        for r in range(streams):
            q_aug = query_tile(r, -kmax)
            base = (r % 2) * tiles if streams > 1 else 0

            def scores(j, slot):
                s_refs[base + slot % tiles][...] = jnp.dot(
                    key_chunk(j), q_aug, preferred_element_type=F32)

            def stages(j0, slot0, carry, count, n_scored):
                acc, lsum = carry
                for u in range(count):
                    if u < n_scored:
                        scores(j0 + u + ATTN_LAG, slot0 + u + ATTN_LAG)
                    p = jnp.exp2(s_refs[base + (slot0 + u) % tiles][...])
                    acc = acc + jnp.dot(value_chunk(j0 + u), p.astype(BF16), preferred_element_type=F32)
                    lsum = lsum + column_sums(p)
                return acc, lsum

            for j in range(ATTN_LAG):
                scores(j, j)
            trips = (n_steps - ATTN_LAG) // ATTN_UNROLL
            carry = lax.fori_loop(
                0, trips, lambda jj, c: stages(jj * ATTN_UNROLL, 0, c, ATTN_UNROLL, ATTN_UNROLL),
                (acc0, lsum0))
            done = trips * ATTN_UNROLL
            rest = n_steps - done
            finish(r, *stages(done, done, carry, rest, rest - ATTN_LAG))

    @pl.when(jnp.logical_not(scores_are_small))
    def _():
        for r in range(streams):
            q_pad = query_tile(r, None)

            def scores(j, s_ref):
                s = jnp.dot(key_chunk(j), q_pad, preferred_element_type=F32)
                s_ref[...] = s
                return jnp.max(s, axis=0, keepdims=True)

            def accumulate(j, s_ref, m, cmax, acc, lsum):
                m_new = jnp.maximum(m, cmax)
                alpha = jnp.exp2(m - m_new)
                p = jnp.exp2(s_ref[...] - m_new)
                acc = alpha * acc + jnp.dot(value_chunk(j), p.astype(BF16), preferred_element_type=F32)
                return m_new, acc, alpha * lsum + column_sums(p)

            def body(jj, carry):
                m, cmax, acc, lsum = carry
                for u in range(depth):
                    j = jj * depth + u
                    cmax_next = scores(jnp.minimum(j + 1, n_steps - 1), s_refs[(u + 1) % depth])
                    m, acc, lsum = accumulate(j, s_refs[u], m, cmax, acc, lsum)
                    cmax = cmax_next
                return m, cmax, acc, lsum

            m0 = jnp.full((1, tq), -1e30, F32)
            cmax0 = scores(0, s_refs[0])
            _, _, acc, lsum = lax.fori_loop(0, n_steps // depth, body, (m0, cmax0, acc0, lsum0))
            finish(r, acc, lsum)


def _attn_call(q, k, v, qn_tiles, kn_tiles, *, heads, q_rows, q_feat, group, name):
    bsz, _, n = q.shape
    n_chunks = v.shape[2]
    n_steps = n // ATTN_TK
    assert n_steps >= ATTN_LAG and n_steps % ATTN_DEPTH == 0 and ATTN_UNROLL % (ATTN_LAG + 1) == 0
    streams = max(1, min(n // ATTN_TQ, ATTN_STAGE_BUDGET // n_steps, ATTN_MAX_TILES))
    assert (n // ATTN_TQ) % streams == 0
    tq = ATTN_TQ * streams
    kn = jnp.max(kn_tiles, axis=2)
    qmax2 = jnp.max(qn_tiles, axis=(2, 3, 4))
    kmax2 = jnp.repeat(jnp.max(kn, axis=(2, 3)), group, axis=1)
    small = (qmax2 * kmax2 * BOUND_SLACK ** 2 <= SAFE_SCORE_BOUND ** 2).astype(jnp.int32).reshape(-1)
    kernel = functools.partial(_attn_kernel, n_steps=n_steps, q_feat=q_feat, heads=heads, streams=streams)
    grid_spec = pltpu.PrefetchScalarGridSpec(
        num_scalar_prefetch=1,
        grid=(bsz, heads, n // tq),
        in_specs=[
            pl.BlockSpec((1, q_rows, tq), lambda b, h, i, _: (b, h, i)),
            pl.BlockSpec((1, 1, n, QK_PAD), lambda b, h, i, _: (b, h // group, 0, 0)),
            pl.BlockSpec((1, 1, n_chunks, V_ROWS, V_CHUNK), lambda b, h, i, _: (b, h // group, 0, 0, 0)),
            pl.BlockSpec((1, 1, 8, LANES), lambda b, h, i, _: (b, h // group, 0, 0)),
        ],
        out_specs=pl.BlockSpec((1, V_ROWS, tq), lambda b, h, i, _: (b, h, i)),
        scratch_shapes=[pltpu.VMEM((ATTN_TK, ATTN_TQ), F32)] * (min(streams, 2) * (ATTN_LAG + 1)),
    )
    return pl.pallas_call(
        kernel,
        grid_spec=grid_spec,
        out_shape=jax.ShapeDtypeStruct((bsz, heads * V_ROWS, n), BF16),
        compiler_params=pltpu.CompilerParams(
            dimension_semantics=("arbitrary", "arbitrary", "arbitrary"), vmem_limit_bytes=VMEM_LIMIT),
        name=name,
    )(small, q, k, v, kn)


def _out_kernel(x_ref, mod_ref, post_g_ref, ya_ref, yb_ref, ag_ref, bg_ref, ma_ref, mb_ref,
                aout_ref, bout_ref, wo_ref, y_ref):
    ga = (ya_ref[0].astype(F32).T * ag_ref[0].astype(F32)).astype(BF16)
    gb = (yb_ref[0].astype(F32).T * bg_ref[0].astype(F32)).astype(BF16)
    pa = jnp.dot(ga, aout_ref[...], preferred_element_type=F32)
    pb = jnp.dot(gb, bout_ref[...], preferred_element_type=F32)
    merged = ma_ref[0].astype(F32) * pa + mb_ref[0].astype(F32) * pb
    z = jnp.dot(merged.astype(BF16), wo_ref[...], preferred_element_type=F32)
    ms = jnp.mean(z * z, axis=-1, keepdims=True)
    zn = z * lax.rsqrt(ms + EPS) * post_g_ref[...]
    y_ref[0] = x_ref[0] + mod_ref[0, 2:3, :] * zn


def _out_call(x, mod3, post_g, ya, yb, ag, bg, ma, mb, a_out, b_out, w_o):
    bsz, n, _ = x.shape
    t = OUT_TILE
    tok = lambda b, i: (b, i, 0)
    feat = lambda b, i: (b, 0, i)
    return pl.pallas_call(
        _out_kernel,
        grid=(bsz, n // t),
        in_specs=[
            pl.BlockSpec((1, t, D_MODEL), tok),
            pl.BlockSpec((1, 3, D_MODEL), lambda b, i: (b, 0, 0)),
            _const_spec((1, D_MODEL)),
            pl.BlockSpec((1, A_WIDTH, t), feat),
            pl.BlockSpec((1, B_WIDTH, t), feat),
            pl.BlockSpec((1, t, A_WIDTH), tok),
            pl.BlockSpec((1, t, B_WIDTH), tok),
            pl.BlockSpec((1, t, D_MODEL), tok),
            pl.BlockSpec((1, t, D_MODEL), tok),
            _const_spec(a_out.shape),
            _const_spec(b_out.shape),
            _const_spec(w_o.shape),
        ],
        out_specs=pl.BlockSpec((1, t, D_MODEL), tok),
        out_shape=jax.ShapeDtypeStruct((bsz, n, D_MODEL), F32),
        compiler_params=pltpu.CompilerParams(
            dimension_semantics=("arbitrary", "arbitrary"), vmem_limit_bytes=VMEM_LIMIT),
        name="out_proj",
    )(x, mod3, post_g, ya, yb, ag, bg, ma, mb, a_out, b_out, w_o)


def _prep_weights(pre_g, w_in, a_q_g, a_k_g, b_q_g, b_q_up, b_kv_g, b_kv_up):
    pts = [0]
    for s in IN_SIZES:
        pts.append(pts[-1] + s)
    seg = lambda i: w_in[:, pts[i]:pts[i + 1]]
    aq, ak, av, ag, bq, bkv, bkr, bg, ma, mb = (seg(i) for i in range(10))
    wt = jnp.concatenate([aq, ak, av, bq, bkr], axis=1).T.astype(BF16)
    wn = jnp.concatenate([bkv, ag, bg, ma, mb], axis=1).astype(BF16)

    qd = B_NOPE + B_ROPE
    bqup = b_q_up.reshape(B_Q_RANK, B_HEADS, qd)
    bqup = jnp.pad(bqup, ((0, 0), (0, 0), (0, QK_PAD - qd))).reshape(B_Q_RANK, B_HEADS * QK_PAD)

    kvup = b_kv_up.reshape(B_KV_RANK, B_HEADS, B_NOPE + B_V)
    k_part = jnp.pad(kvup[:, :, :B_NOPE], ((0, 0), (0, 0), (0, QK_PAD - B_NOPE)))
    eye = jnp.eye(LANES, QK_PAD, k=B_NOPE, dtype=F32)[:, None, :]
    eye = jnp.where(jnp.arange(LANES)[:, None, None] <= B_ROPE, eye, 0.0)
    r_part = jnp.broadcast_to(eye, (LANES, B_HEADS, QK_PAD))
    waug = jnp.concatenate([k_part, r_part], axis=0).reshape(B_KV_RANK + LANES, B_HEADS * QK_PAD)
    bvup = kvup[:, :, B_NOPE:].reshape(B_KV_RANK, B_HEADS * B_V).T

    return {
        "pre_g": pre_g.reshape(1, D_MODEL),
        "wt": wt, "wn": wn,
        "aqg": a_q_g.reshape(A_HEAD_DIM, 1), "akg": a_k_g.reshape(A_HEAD_DIM, 1),
        "bqg": b_q_g.reshape(B_Q_RANK, 1), "bkvg": b_kv_g.reshape(1, B_KV_RANK),
        "bqup": bqup.T.astype(BF16), "waug": waug.astype(BF16), "bvup": bvup.astype(BF16),
    }


def _layer(x, mod3, pw, tables, post_g, a_out, b_out, w_o):
    qa, ka, va, ag, bg, ma, mb, qb, kb, vb, kna, knb, qna, qnb = _proj_call(x, mod3, pw, tables)
    ya = _attn_call(qa, ka, va, qna, kna, heads=A_HEADS, q_rows=A_QROWS, q_feat=A_HEAD_DIM,
                    group=A_HEADS // A_KV_HEADS, name="attn_gqa")
    yb = _attn_call(qb, kb, vb, qnb, knb, heads=B_HEADS, q_rows=QK_PAD, q_feat=B_NOPE + B_ROPE,
                    group=1, name="attn_mla")
    return _out_call(x, mod3, post_g, ya, yb, ag, bg, ma, mb, a_out, b_out, w_o)


def kernel(x_prompt, x_sample, c_prompt, c_sample, ada_w, ada_b, pre_norm_g, post_norm_g, w_in,
           a_q_norm_g, a_k_norm_g, b_q_norm_g, b_q_up, b_kv_norm_g, b_kv_up, a_out, b_out, w_o):
    assert ada_w.shape[0] == 1, "single layer"
    bp, bs = c_prompt.shape[0], c_sample.shape[0]
    rows = -(-(bp + bs) // 8) * 8
    c_all = jnp.concatenate([c_prompt, c_sample, jnp.zeros((rows - bp - bs, D_MODEL), F32)], axis=0)
    mod = _mod_call(c_all, ada_w[0].astype(BF16), ada_b[0].reshape(1, 3 * D_MODEL))
    mod3 = mod.reshape(rows, 3, D_MODEL)

    pw = _prep_weights(pre_norm_g[0], w_in[0], a_q_norm_g[0], a_k_norm_g[0], b_q_norm_g[0],
                       b_q_up[0], b_kv_norm_g[0], b_kv_up[0])
    post_g = post_norm_g[0].reshape(1, D_MODEL)
    a_o, b_o, w_oo = a_out[0].astype(BF16), b_out[0].astype(BF16), w_o[0].astype(BF16)

    n_max = max(x_prompt.shape[1], x_sample.shape[1])
    tables = _rope_tables_t(n_max, A_HEAD_DIM) + _rope_tables_t(n_max, B_ROPE)
    y_prompt = _layer(x_prompt, mod3[:bp], pw, tables, post_g, a_o, b_o, w_oo)
    y_sample = _layer(x_sample, mod3[bp:bp + bs], pw, tables, post_g, a_o, b_o, w_oo)
    return (y_prompt, y_sample)
```
